```python
import jax, jax.numpy as jnp
from jax import lax
import numpy as np

D_MODEL = 1024
BATCH = 8
SEQ = 4096
DEPTH = 4
DEC_BATCH = 4
DEC_SEQ = 4096
PAST_LEN = 128

MIX_WIDTH = D_MODEL
ATTN_HEAD_DIM = 64
ATTN_WIDTH = MIX_WIDTH // 2
ATTN_HEADS = ATTN_WIDTH // ATTN_HEAD_DIM
ATTN_KV_HEADS = 2
ATTN_GROUP = ATTN_HEADS // ATTN_KV_HEADS
KV_WIDTH = ATTN_KV_HEADS * ATTN_HEAD_DIM
WINDOW = 128
BLOCK = 128
RET_WIDTH = MIX_WIDTH - ATTN_WIDTH
RET_HEAD_DIM = 128
RET_HEADS = RET_WIDTH // RET_HEAD_DIM
RET_CHUNK = 128
D_FF = 4 * D_MODEL
N_MOD = 6
EPS = 1e-6
NEG_INF = -1e30
IN_SIZES = (ATTN_WIDTH, KV_WIDTH, KV_WIDTH, RET_WIDTH, RET_WIDTH, RET_WIDTH, RET_WIDTH)
IN_WIDTH = sum(IN_SIZES)
IN_OFFSETS = tuple(int(o) for o in np.cumsum(IN_SIZES)[:-1])

kernel_name = "hymba_swa_retention_adaln_encoder"


def rms_norm(x, g):
    xf = x.astype(jnp.float32)
    y = xf * lax.rsqrt(jnp.mean(xf * xf, axis=-1, keepdims=True) + EPS)
    return (y * g.astype(jnp.float32)).astype(x.dtype)


def alibi_slopes():
    return jnp.exp2(-8.0 * (jnp.arange(ATTN_HEADS, dtype=jnp.float32) + 1.0) / ATTN_HEADS)


def windowed_attention(q, k, v, sink):
    B, S = q.shape[0], q.shape[1]
    N = S // BLOCK
    qb = q.astype(jnp.float32).reshape(B, N, BLOCK, ATTN_KV_HEADS, ATTN_GROUP, ATTN_HEAD_DIM)
    pad = ((0, 0), (BLOCK, BLOCK), (0, 0), (0, 0))
    kp = jnp.pad(k.astype(jnp.float32), pad).reshape(B, N + 2, BLOCK, ATTN_KV_HEADS, ATTN_HEAD_DIM)
    vp = jnp.pad(v.astype(jnp.float32), pad).reshape(B, N + 2, BLOCK, ATTN_KV_HEADS, ATTN_HEAD_DIM)
    kb = jnp.concatenate([kp[:, :-2], kp[:, 1:-1], kp[:, 2:]], axis=2)
    vb = jnp.concatenate([vp[:, :-2], vp[:, 1:-1], vp[:, 2:]], axis=2)
    s = jnp.einsum('bnikgd,bnjkd->bnkgij', qb, kb) * (ATTN_HEAD_DIM ** -0.5)
    i = jnp.arange(BLOCK)[:, None]
    j = jnp.arange(3 * BLOCK)[None, :]
    dist = jnp.abs(j - BLOCK - i)
    key_pos = jnp.arange(N)[:, None, None] * BLOCK - BLOCK + j[None]
    valid = (dist <= WINDOW)[None] & (key_pos >= 0) & (key_pos < S)
    slopes = alibi_slopes().reshape(ATTN_KV_HEADS, ATTN_GROUP)
    bias = -slopes[:, :, None, None] * dist.astype(jnp.float32)
    s = jnp.where(valid[None, :, None, None], s + bias, NEG_INF)
    sinkf = sink.astype(jnp.float32).reshape(ATTN_KV_HEADS, ATTN_GROUP)[:, :, None, None]
    m = jnp.maximum(jnp.max(s, axis=-1, keepdims=True), sinkf)
    p = jnp.exp(s - m)
    denom = jnp.sum(p, axis=-1, keepdims=True) + jnp.exp(sinkf - m)
    o = jnp.einsum('bnkgij,bnjkd->bnikgd', p / denom, vb)
    return o.reshape(B, S, ATTN_WIDTH)


def retention_direction(q, k, v, log_g, strict):
    B, S, H, d = q.shape
    C = RET_CHUNK
    N = S // C
    qc = q.reshape(B, N, C, H, d)
    kc = k.reshape(B, N, C, H, d) * (d ** -0.5)
    vc = v.reshape(B, N, C, H, d)
    pos = jnp.arange(C, dtype=jnp.float32)
    diff = pos[:, None] - pos[None, :]
    mask = (diff > 0) if strict else (diff >= 0)
    dmat = jnp.where(mask[None], jnp.exp(jnp.maximum(diff, 0.0)[None] * log_g[:, None, None]), 0.0)
    scores = jnp.einsum('bnihd,bnjhd->bnhij', qc, kc) * dmat
    inner = jnp.einsum('bnhij,bnjhe->bnihe', scores, vc)
    k_dec = kc * jnp.exp((C - 1.0 - pos)[:, None] * log_g[None, :])[:, :, None]
    kv = jnp.einsum('bnjhd,bnjhe->nbhde', k_dec, vc)
    chunk_decay = jnp.exp(C * log_g)[None, :, None, None]

    def step(r, kv_n):
        return r * chunk_decay + kv_n, r

    _, r_prev = lax.scan(step, jnp.zeros_like(kv[0]), kv)
    q_dec = qc * jnp.exp((pos + 1.0)[:, None] * log_g[None, :])[:, :, None]
    cross = jnp.einsum('bnihd,nbhde->bnihe', q_dec, r_prev)
    return (inner + cross).reshape(B, S, H, d)


def bidirectional_retention(q, k, v, g, decay_fwd, decay_bwd):
    B, S = q.shape[0], q.shape[1]
    shp = (B, S, RET_HEADS, RET_HEAD_DIM)
    qf = q.astype(jnp.float32).reshape(shp)
    kf = k.astype(jnp.float32).reshape(shp)
    vf = v.astype(jnp.float32).reshape(shp)
    lg_f = jax.nn.log_sigmoid(decay_fwd.astype(jnp.float32))
    lg_b = jax.nn.log_sigmoid(decay_bwd.astype(jnp.float32))
    y_f = retention_direction(qf, kf, vf, lg_f, False)
    y_b = jnp.flip(retention_direction(jnp.flip(qf, 1), jnp.flip(kf, 1), jnp.flip(vf, 1), lg_b, True), 1)
    y = y_f + y_b
    mu = jnp.mean(y, axis=-1, keepdims=True)
    yc = y - mu
    yn = yc * lax.rsqrt(jnp.mean(yc * yc, axis=-1, keepdims=True) + EPS)
    return jax.nn.silu(g.astype(jnp.float32)) * yn.reshape(B, S, RET_WIDTH)


def run_trunk(x, c, w_ada, b_ada, norm1_g, w_in, attn_sink, ret_decay_fwd, ret_decay_bwd,
              w_out, norm2_g, w_mlp1, w_mlp2, final_g):
    B, S, _ = x.shape
    c_act = jax.nn.silu(c)
    for l in range(DEPTH):
        mod = c_act @ w_ada[l] + b_ada[l]
        sh1, sc1, g1, sh2, sc2, g2 = [m[:, None, :] for m in jnp.split(mod, N_MOD, axis=-1)]
        h = rms_norm(x, norm1_g[l]) * (1.0 + sc1) + sh1
        proj = h @ w_in[l]
        q_a, k_a, v_a, q_r, k_r, v_r, g_r = jnp.split(proj, IN_OFFSETS, axis=-1)
        o_a = windowed_attention(q_a.reshape(B, S, ATTN_HEADS, ATTN_HEAD_DIM),
                                 k_a.reshape(B, S, ATTN_KV_HEADS, ATTN_HEAD_DIM),
                                 v_a.reshape(B, S, ATTN_KV_HEADS, ATTN_HEAD_DIM),
                                 attn_sink[l])
        o_r = bidirectional_retention(q_r, k_r, v_r, g_r, ret_decay_fwd[l], ret_decay_bwd[l])
        mix = jnp.concatenate([o_a, o_r], axis=-1).astype(x.dtype)
        x = x + g1 * (mix @ w_out[l])
        h = rms_norm(x, norm2_g[l]) * (1.0 + sc2) + sh2
        x = x + g2 * (jnp.square(jax.nn.relu(h @ w_mlp1[l])) @ w_mlp2[l])
    return rms_norm(x, final_g)


def setup_inputs(seed: int = 0) -> dict:
    key = jax.random.key(seed)
    ks = jax.random.split(key, 16)
    f32 = jnp.float32
    base_decay = jnp.log(jnp.exp2(5.0 + jnp.arange(RET_HEADS, dtype=f32)) - 1.0)
    return {
        "x_prompt": jax.random.normal(ks[0], (BATCH, SEQ, D_MODEL), f32),
        "x_sample": jax.random.normal(ks[1], (DEC_BATCH, DEC_SEQ, D_MODEL), f32),
        "c_prompt": jax.random.normal(ks[2], (BATCH, D_MODEL), f32),
        "c_sample": jax.random.normal(ks[3], (DEC_BATCH, D_MODEL), f32),
        "w_ada": jax.random.normal(ks[4], (DEPTH, D_MODEL, N_MOD * D_MODEL), f32) * 0.02,
        "b_ada": jax.random.normal(ks[5], (DEPTH, N_MOD * D_MODEL), f32) * 0.01,
        "norm1_g": 1.0 + 0.02 * jax.random.normal(ks[6], (DEPTH, D_MODEL), f32),
        "w_in": jax.random.normal(ks[7], (DEPTH, D_MODEL, IN_WIDTH), f32) * D_MODEL ** -0.5,
        "attn_sink": 0.5 * jax.random.normal(ks[8], (DEPTH, ATTN_HEADS), f32),
        "ret_decay_fwd": base_decay[None] + 0.1 * jax.random.normal(ks[9], (DEPTH, RET_HEADS), f32),
        "ret_decay_bwd": base_decay[None] + 0.1 * jax.random.normal(ks[10], (DEPTH, RET_HEADS), f32),
        "w_out": jax.random.normal(ks[11], (DEPTH, MIX_WIDTH, D_MODEL), f32) * MIX_WIDTH ** -0.5,
        "norm2_g": 1.0 + 0.02 * jax.random.normal(ks[12], (DEPTH, D_MODEL), f32),
        "w_mlp1": jax.random.normal(ks[13], (DEPTH, D_MODEL, D_FF), f32) * D_MODEL ** -0.5,
        "w_mlp2": jax.random.normal(ks[14], (DEPTH, D_FF, D_MODEL), f32) * D_FF ** -0.5,
        "final_g": 1.0 + 0.02 * jax.random.normal(ks[15], (D_MODEL,), f32),
    }


def reference(x_prompt, x_sample, c_prompt, c_sample, w_ada, b_ada, norm1_g, w_in, attn_sink,
              ret_decay_fwd, ret_decay_bwd, w_out, norm2_g, w_mlp1, w_mlp2, final_g):
    y_prompt = run_trunk(x_prompt, c_prompt, w_ada, b_ada, norm1_g, w_in, attn_sink, ret_decay_fwd,
                         ret_decay_bwd, w_out, norm2_g, w_mlp1, w_mlp2, final_g)
    y_sample = run_trunk(x_sample, c_sample, w_ada, b_ada, norm1_g, w_in, attn_sink, ret_decay_fwd,
                         ret_decay_bwd, w_out, norm2_g, w_mlp1, w_mlp2, final_g)
    return (y_prompt, y_sample)
```

```python
import functools

import jax
import jax.numpy as jnp
from jax import lax
from jax.experimental import pallas as pl
from jax.experimental.pallas import tpu as pltpu

D_MODEL = 1024
DEPTH = 4
ATTN_HEAD_DIM = 64
ATTN_WIDTH = 512
ATTN_HEADS = 8
ATTN_KV_HEADS = 2
ATTN_GROUP = 4
KV_WIDTH = 128
WINDOW = 128
BLOCK = 128
RET_WIDTH = 512
RET_HEAD_DIM = 128
RET_HEADS = 4
D_FF = 4 * D_MODEL
N_MOD = 6
EPS = 1e-6
NEG_INF = -1e30
IN_WIDTH = ATTN_WIDTH + 2 * KV_WIDTH + 4 * RET_WIDTH

ROW_TILE = 512
FF_CHUNK = 1024
ADA_COLS = 1536
MOD_ROWS = 16
VMEM_LIMIT = 56 * 1024 * 1024

_BF16 = jnp.bfloat16
_F32 = jnp.float32
_NT = (((1,), (1,)), ((), ()))
_TN = (((0,), (0,)), ((), ()))


def _split_bf16(a):
    hi = a.astype(_BF16)
    lo = (a - hi.astype(_F32)).astype(_BF16)
    return hi, lo


def _ada_kernel(c_ref, w_ref, b_ref, o_ref):
    c = c_ref[...]
    act = c * jax.nn.sigmoid(c)
    a_hi, a_lo = _split_bf16(act)
    w_hi, w_lo = _split_bf16(w_ref[0])
    acc = jnp.dot(a_hi, w_hi, preferred_element_type=_F32)
    acc += jnp.dot(a_hi, w_lo, preferred_element_type=_F32)
    acc += jnp.dot(a_lo, w_hi, preferred_element_type=_F32)
    o_ref[0] = acc + b_ref[0]


def _ada_modulation(c_all, w_ada, b_ada):
    n_cols = N_MOD * D_MODEL
    return pl.pallas_call(
        _ada_kernel,
        grid=(DEPTH, n_cols // ADA_COLS),
        in_specs=[
            pl.BlockSpec((MOD_ROWS, D_MODEL), lambda l, j: (0, 0)),
            pl.BlockSpec((1, D_MODEL, ADA_COLS), lambda l, j: (l, 0, j)),
            pl.BlockSpec((1, 1, ADA_COLS), lambda l, j: (l, 0, j)),
        ],
        out_specs=pl.BlockSpec((1, MOD_ROWS, ADA_COLS), lambda l, j: (l, 0, j)),
        out_shape=jax.ShapeDtypeStruct((DEPTH, MOD_ROWS, n_cols), _F32),
        compiler_params=pltpu.CompilerParams(
            dimension_semantics=("arbitrary", "arbitrary"), vmem_limit_bytes=VMEM_LIMIT),
        name="ada_modulation",
    )(c_all, w_ada, b_ada.reshape(DEPTH, 1, n_cols))


_IN_SPLITS = (ATTN_WIDTH, KV_WIDTH, KV_WIDTH, RET_WIDTH, RET_WIDTH, RET_WIDTH, RET_WIDTH)


def _modulated_norm(x, gain, shift, scale):
    ms = jnp.mean(x * x, axis=-1, keepdims=True)
    return (x * lax.rsqrt(ms + EPS) * gain) * (1.0 + scale) + shift


def _inproj_kernel(x_ref, mod_ref, g_ref, w_ref, qa_ref, ka_ref, va_ref, qr_ref, kr_ref, vr_ref, gr_ref):
    h = _modulated_norm(x_ref[0], g_ref[...], mod_ref[0, 0:1, :], mod_ref[0, 1:2, :]).astype(_BF16)
    outs = (qa_ref, ka_ref, va_ref, qr_ref, kr_ref, vr_ref, gr_ref)
    off = 0
    for idx, (o_ref, width) in enumerate(zip(outs, _IN_SPLITS)):
        p = jnp.dot(h, w_ref[:, off:off + width], preferred_element_type=_F32)
        if idx == 0:
            p = p * (ATTN_HEAD_DIM ** -0.5)
        o_ref[0] = p.astype(_BF16)
        off += width


def _in_projection(x, mod, gain, w_in):
    B, S, _ = x.shape
    row = lambda b, i: (b, i, 0)
    return pl.pallas_call(
        _inproj_kernel,
        grid=(B, S // ROW_TILE),
        in_specs=[
            pl.BlockSpec((1, ROW_TILE, D_MODEL), row),
            pl.BlockSpec((1, N_MOD, D_MODEL), lambda b, i: (b, 0, 0)),
            pl.BlockSpec((1, D_MODEL), lambda b, i: (0, 0)),
            pl.BlockSpec((D_MODEL, IN_WIDTH), lambda b, i: (0, 0)),
        ],
        out_specs=[pl.BlockSpec((1, ROW_TILE, w), row) for w in _IN_SPLITS],
        out_shape=[jax.ShapeDtypeStruct((B, S, w), _BF16) for w in _IN_SPLITS],
        compiler_params=pltpu.CompilerParams(
            dimension_semantics=("arbitrary", "arbitrary"), vmem_limit_bytes=VMEM_LIMIT),
        name="in_projection",
    )(x, mod, gain, w_in)


def _log_decay(decay_row):
    return jnp.minimum(decay_row, 0.0) - jnp.log1p(jnp.exp(-jnp.abs(decay_row)))


def _state_kernel(dec_ref, kf_ref, vf_ref, kb_ref, vb_ref, rf_ref, rb_ref, sf_ref, sb_ref):
    j = pl.program_id(1)

    @pl.when(j == 0)
    def _():
        sf_ref[...] = jnp.zeros_like(sf_ref)
        sb_ref[...] = jnp.zeros_like(sb_ref)

    pos = lax.broadcasted_iota(jnp.int32, (BLOCK, 1), 0).astype(_F32)
    k_scale = RET_HEAD_DIM ** -0.5
    for h in range(RET_HEADS):
        cols = slice(h * RET_HEAD_DIM, (h + 1) * RET_HEAD_DIM)
        for d, (k_ref, v_ref, r_ref, s_ref) in enumerate(
                ((kf_ref, vf_ref, rf_ref, sf_ref), (kb_ref, vb_ref, rb_ref, sb_ref))):
            lg = _log_decay(dec_ref[d:d + 1, cols])
            expo = (BLOCK - 1.0 - pos) if d == 0 else pos
            k_dec = (k_ref[0, :, cols].astype(_F32) * (k_scale * jnp.exp(expo * lg))).astype(_BF16)
            kv = lax.dot_general(k_dec, v_ref[0, :, cols], _TN, preferred_element_type=_F32)
            state = s_ref[h]
            r_ref[0, 0, h] = state.astype(_BF16)
            s_ref[h] = state * jnp.exp(float(BLOCK) * lg) + kv


def _retention_states(k_r, v_r, decays):
    B, S, _ = k_r.shape
    N = S // BLOCK
    fwd = lambda b, j: (b, j, 0)
    bwd = lambda b, j: (b, N - 1 - j, 0)
    st_shape = jax.ShapeDtypeStruct((B, N, RET_HEADS, RET_HEAD_DIM, RET_HEAD_DIM), _BF16)
    st_block = (1, 1, RET_HEADS, RET_HEAD_DIM, RET_HEAD_DIM)
    return pl.pallas_call(
        _state_kernel,
        grid=(B, N),
        in_specs=[
            pl.BlockSpec((2, RET_WIDTH), lambda b, j: (0, 0)),
            pl.BlockSpec((1, BLOCK, RET_WIDTH), fwd),
            pl.BlockSpec((1, BLOCK, RET_WIDTH), fwd),
            pl.BlockSpec((1, BLOCK, RET_WIDTH), bwd),
            pl.BlockSpec((1, BLOCK, RET_WIDTH), bwd),
        ],
        out_specs=[
            pl.BlockSpec(st_block, lambda b, j: (b, j, 0, 0, 0)),
            pl.BlockSpec(st_block, lambda b, j: (b, N - 1 - j, 0, 0, 0)),
        ],
        out_shape=[st_shape, st_shape],
        scratch_shapes=[
            pltpu.VMEM((RET_HEADS, RET_HEAD_DIM, RET_HEAD_DIM), _F32),
            pltpu.VMEM((RET_HEADS, RET_HEAD_DIM, RET_HEAD_DIM), _F32),
        ],
        compiler_params=pltpu.CompilerParams(
            dimension_semantics=("arbitrary", "arbitrary"), vmem_limit_bytes=VMEM_LIMIT),
        name="retention_states",
    )(decays, k_r, v_r, k_r, v_r)


def _mix_kernel(sink_ref, dec_ref, qa_ref, kp_ref, kc_ref, kn_ref, vp_ref, vc_ref, vn_ref,
                qr_ref, kr_ref, vr_ref, gr_ref, rf_ref, rb_ref, o_ref,
                bias_ref, dcomb_ref, qdec_ref, *, n_blocks):
    n = pl.program_id(1)

    @pl.when(n == 0)
    def _():
        i = lax.broadcasted_iota(jnp.int32, (BLOCK, 3 * BLOCK), 0)
        jj = lax.broadcasted_iota(jnp.int32, (BLOCK, 3 * BLOCK), 1)
        dist = jnp.abs(jj - BLOCK - i)
        distf = dist.astype(_F32)
        for hh in range(ATTN_HEADS):
            slope = 2.0 ** (-8.0 * (hh + 1.0) / ATTN_HEADS)
            kh, g = divmod(hh, ATTN_GROUP)
            bias_ref[kh, g * BLOCK:(g + 1) * BLOCK, :] = jnp.where(dist <= WINDOW, -slope * distf, NEG_INF)
        ri = lax.broadcasted_iota(jnp.int32, (BLOCK, BLOCK), 0)
        rj = lax.broadcasted_iota(jnp.int32, (BLOCK, BLOCK), 1)
        diff = (ri - rj).astype(_F32)
        pos = lax.broadcasted_iota(jnp.int32, (BLOCK, 1), 0).astype(_F32)
        for h in range(RET_HEADS):
            cols = slice(h * RET_HEAD_DIM, (h + 1) * RET_HEAD_DIM)
            lg_f = _log_decay(dec_ref[0:1, cols])
            lg_b = _log_decay(dec_ref[1:2, cols])
            d_f = jnp.exp(jnp.maximum(diff, 0.0) * lg_f)
            d_b = jnp.exp(jnp.maximum(-diff, 0.0) * lg_b)
            dcomb_ref[h] = jnp.where(diff >= 0, d_f, d_b) * (RET_HEAD_DIM ** -0.5)
            qdec_ref[0, h] = jnp.broadcast_to(jnp.exp((pos + 1.0) * lg_f), (BLOCK, RET_HEAD_DIM))
            qdec_ref[1, h] = jnp.broadcast_to(jnp.exp((float(BLOCK) - pos) * lg_b), (BLOCK, RET_HEAD_DIM))

    kband = jnp.concatenate([kp_ref[0], kc_ref[0], kn_ref[0]], axis=0)
    vband = jnp.concatenate([vp_ref[0], vc_ref[0], vn_ref[0]], axis=0)
    q = qa_ref[0]
    col = lax.broadcasted_iota(jnp.int32, (1, 3 * BLOCK), 1)
    outside = ((col < BLOCK) & (n == 0)) | ((col >= 2 * BLOCK) & (n == n_blocks - 1))
    edge = jnp.where(outside, NEG_INF, 0.0)
    ones = jnp.ones((3 * BLOCK, BLOCK), _BF16)
    lane = lax.broadcasted_iota(jnp.int32, (BLOCK, BLOCK), 1)
    pieces = []
    for kh in range(ATTN_KV_HEADS):
        heads = [kh * ATTN_GROUP + g for g in range(ATTN_GROUP)]
        qg = jnp.concatenate([q[:, hh * ATTN_HEAD_DIM:(hh + 1) * ATTN_HEAD_DIM] for hh in heads], axis=0)
        kk = kband[:, kh * ATTN_HEAD_DIM:(kh + 1) * ATTN_HEAD_DIM]
        vv = vband[:, kh * ATTN_HEAD_DIM:(kh + 1) * ATTN_HEAD_DIM]
        s = lax.dot_general(qg, kk, _NT, preferred_element_type=_F32)
        s = s + bias_ref[kh] + edge
        sink_col = jnp.concatenate(
            [jnp.full((BLOCK, 1), sink_ref[hh], _F32) for hh in heads], axis=0)
        m = jnp.maximum(jnp.max(s, axis=-1, keepdims=True), sink_col)
        p = jnp.exp(s - m).astype(_BF16)
        rhs = jnp.concatenate([vv, vv, ones], axis=1)
        pv = jnp.dot(p, rhs, preferred_element_type=_F32)
        denom = pv[:, BLOCK:] + jnp.exp(sink_col - m)
        o = pv[:, :BLOCK] / denom
        for pair in range(ATTN_GROUP // 2):
            even = o[(2 * pair) * BLOCK:(2 * pair + 1) * BLOCK]
            odd = o[(2 * pair + 1) * BLOCK:(2 * pair + 2) * BLOCK]
            pieces.append(jnp.where(lane < ATTN_HEAD_DIM, even, odd))

    for h in range(RET_HEADS):
        cols = slice(h * RET_HEAD_DIM, (h + 1) * RET_HEAD_DIM)
        qh = qr_ref[0, :, cols]
        s = lax.dot_general(qh, kr_ref[0, :, cols], _NT, preferred_element_type=_F32)
        inner = jnp.dot((s * dcomb_ref[h]).astype(_BF16), vr_ref[0, :, cols], preferred_element_type=_F32)
        qf32 = qh.astype(_F32)
        q_f = (qf32 * qdec_ref[0, h]).astype(_BF16)
        q_b = (qf32 * qdec_ref[1, h]).astype(_BF16)
        y = inner + jnp.dot(q_f, rf_ref[0, 0, h], preferred_element_type=_F32)
        y = y + jnp.dot(q_b, rb_ref[0, 0, h], preferred_element_type=_F32)
        yc = y - jnp.mean(y, axis=-1, keepdims=True)
        yn = yc * lax.rsqrt(jnp.mean(yc * yc, axis=-1, keepdims=True) + EPS)
        gate = gr_ref[0, :, cols].astype(_F32)
        pieces.append(gate * jax.nn.sigmoid(gate) * yn)

    o_ref[0] = jnp.concatenate(pieces, axis=1).astype(_BF16)


def _token_mixing(qa, ka, va, qr, kr, vr, gr, rf, rb, sink, decays):
    B, S, _ = qa.shape
    N = S // BLOCK
    cur = lambda b, n: (b, n, 0)
    prev = lambda b, n: (b, jnp.maximum(n - 1, 0), 0)
    nxt = lambda b, n: (b, jnp.minimum(n + 1, N - 1), 0)
    st_block = (1, 1, RET_HEADS, RET_HEAD_DIM, RET_HEAD_DIM)
    st_map = lambda b, n: (b, n, 0, 0, 0)
    kv_spec = lambda m: pl.BlockSpec((1, BLOCK, KV_WIDTH), m)
    wide = lambda w: pl.BlockSpec((1, BLOCK, w), cur)
    return pl.pallas_call(
        functools.partial(_mix_kernel, n_blocks=N),
        grid=(B, N),
        in_specs=[
            pl.BlockSpec(memory_space=pltpu.SMEM),
            pl.BlockSpec((2, RET_WIDTH), lambda b, n: (0, 0)),
            wide(ATTN_WIDTH),
            kv_spec(prev), kv_spec(cur), kv_spec(nxt),
            kv_spec(prev), kv_spec(cur), kv_spec(nxt),
            wide(RET_WIDTH), wide(RET_WIDTH), wide(RET_WIDTH), wide(RET_WIDTH),
            pl.BlockSpec(st_block, st_map), pl.BlockSpec(st_block, st_map),
        ],
        out_specs=pl.BlockSpec((1, BLOCK, D_MODEL), cur),
        out_shape=jax.ShapeDtypeStruct((B, S, D_MODEL), _BF16),
        scratch_shapes=[
            pltpu.VMEM((ATTN_KV_HEADS, ATTN_GROUP * BLOCK, 3 * BLOCK), _F32),
            pltpu.VMEM((RET_HEADS, BLOCK, BLOCK), _F32),
            pltpu.VMEM((2, RET_HEADS, BLOCK, RET_HEAD_DIM), _F32),
        ],
        compiler_params=pltpu.CompilerParams(
            dimension_semantics=("arbitrary", "arbitrary"), vmem_limit_bytes=VMEM_LIMIT),
        name="token_mixing",
    )(sink, decays, qa, ka, ka, ka, va, va, va, qr, kr, vr, gr, rf, rb)


def _mlp_kernel(x_ref, mix_ref, mod_ref, g2_ref, gf_ref, wo_ref, w1_ref, w2_ref, o_ref, *, final):
    x1 = x_ref[0] + mod_ref[0, 2:3, :] * jnp.dot(mix_ref[0], wo_ref[...], preferred_element_type=_F32)
    h = _modulated_norm(x1, g2_ref[...], mod_ref[0, 3:4, :], mod_ref[0, 4:5, :]).astype(_BF16)
    acc = jnp.zeros((ROW_TILE, D_MODEL), _F32)
    for c in range(D_FF // FF_CHUNK):
        cols = slice(c * FF_CHUNK, (c + 1) * FF_CHUNK)
        a = jnp.dot(h, w1_ref[:, cols], preferred_element_type=_F32)
        a = jnp.square(jnp.maximum(a, 0.0)).astype(_BF16)
        acc = acc + jnp.dot(a, w2_ref[cols, :], preferred_element_type=_F32)
    x2 = x1 + mod_ref[0, 5:6, :] * acc
    if final:
        ms = jnp.mean(x2 * x2, axis=-1, keepdims=True)
        x2 = x2 * lax.rsqrt(ms + EPS) * gf_ref[...]
    o_ref[0] = x2


def _channel_mixing(x, mix, mod, gain2, gain_final, w_out, w_mlp1, w_mlp2, final):
    B, S, _ = x.shape
    row = lambda b, i: (b, i, 0)
    const = lambda b, i: (0, 0)
    resident = dict(pipeline_mode=pl.Buffered(1))
    return pl.pallas_call(
        functools.partial(_mlp_kernel, final=final),
        grid=(B, S // ROW_TILE),
        in_specs=[
            pl.BlockSpec((1, ROW_TILE, D_MODEL), row),
            pl.BlockSpec((1, ROW_TILE, D_MODEL), row),
            pl.BlockSpec((1, N_MOD, D_MODEL), lambda b, i: (b, 0, 0)),
            pl.BlockSpec((1, D_MODEL), const),
            pl.BlockSpec((1, D_MODEL), const),
            pl.BlockSpec((D_MODEL, D_MODEL), const, **resident),
            pl.BlockSpec((D_MODEL, D_FF), const, **resident),
            pl.BlockSpec((D_FF, D_MODEL), const, **resident),
        ],
        out_specs=pl.BlockSpec((1, ROW_TILE, D_MODEL), row),
        out_shape=jax.ShapeDtypeStruct((B, S, D_MODEL), _F32),
        compiler_params=pltpu.CompilerParams(
            dimension_semantics=("arbitrary", "arbitrary"), vmem_limit_bytes=VMEM_LIMIT),
        name="channel_mixing",
    )(x, mix, mod, gain2, gain_final, w_out, w_mlp1, w_mlp2)


def _run_trunk(x, mod_rows, norm1_g, w_in, attn_sink, decays, w_out, norm2_g, w_mlp1, w_mlp2, final_g):
    B = x.shape[0]
    for l in range(DEPTH):
        mod = mod_rows[l].reshape(B, N_MOD, D_MODEL)
        qa, ka, va, qr, kr, vr, gr = _in_projection(x, mod, norm1_g[l:l + 1], w_in[l])
        rf, rb = _retention_states(kr, vr, decays[l])
        mix = _token_mixing(qa, ka, va, qr, kr, vr, gr, rf, rb, attn_sink[l], decays[l])
        x = _channel_mixing(x, mix, mod, norm2_g[l:l + 1], final_g, w_out[l], w_mlp1[l], w_mlp2[l],
                            final=(l == DEPTH - 1))
    return x


def kernel(x_prompt, x_sample, c_prompt, c_sample, w_ada, b_ada, norm1_g, w_in, attn_sink,
           ret_decay_fwd, ret_decay_bwd, w_out, norm2_g, w_mlp1, w_mlp2, final_g):
    bp, bs = x_prompt.shape[0], x_sample.shape[0]
    c_all = jnp.concatenate(
        [c_prompt, c_sample, jnp.zeros((MOD_ROWS - bp - bs, D_MODEL), _F32)], axis=0)
    mod_all = _ada_modulation(c_all, w_ada, b_ada)
    decays = jnp.stack([jnp.repeat(ret_decay_fwd, RET_HEAD_DIM, axis=-1),
                        jnp.repeat(ret_decay_bwd, RET_HEAD_DIM, axis=-1)], axis=1)
    shared = (norm1_g, w_in.astype(_BF16), attn_sink, decays, w_out.astype(_BF16), norm2_g,
              w_mlp1.astype(_BF16), w_mlp2.astype(_BF16), final_g.reshape(1, D_MODEL))
    y_prompt = _run_trunk(x_prompt, mod_all[:, :bp], *shared)
    y_sample = _run_trunk(x_sample, mod_all[:, bp:bp + bs], *shared)
    return (y_prompt, y_sample)
```

```python
import functools

import jax
import jax.numpy as jnp
from jax import lax
from jax.experimental import pallas as pl
from jax.experimental.pallas import tpu as pltpu

D_MODEL = 1024
DEPTH = 4
ATTN_HEAD_DIM = 64
ATTN_WIDTH = 512
ATTN_HEADS = 8
ATTN_KV_HEADS = 2
ATTN_GROUP = 4
KV_WIDTH = 128
WINDOW = 128
BLOCK = 128
BAND = 3 * BLOCK
RET_WIDTH = 512
RET_HEAD_DIM = 128
RET_HEADS = 4
D_FF = 4 * D_MODEL
N_MOD = 6
EPS = 1e-6
NEG_INF = -1e30
LOG2_E = 1.4426950408889634
OFF_QA, OFF_KA, OFF_VA, OFF_QR, OFF_KR, OFF_VR, OFF_GR, IN_WIDTH = 0, 512, 640, 768, 1280, 1792, 2304, 2816

ROW_TILE = 512
FF_CHUNK = 1024
ADA_COLS = 1536
MOD_ROWS = 16
STATE_CHUNKS = 4
VMEM_LIMIT = 56 * 1024 * 1024

_BF16 = jnp.bfloat16
_F32 = jnp.float32
_NT = (((1,), (1,)), ((), ()))


def _split_bf16(a):
    hi = a.astype(_BF16)
    lo = (a - hi.astype(_F32)).astype(_BF16)
    return hi, lo


def _params(n_axes=2):
    return pltpu.CompilerParams(
        dimension_semantics=("arbitrary",) * n_axes, vmem_limit_bytes=VMEM_LIMIT)


def _ada_kernel(c_ref, w_ref, b_ref, o_ref):
    c = c_ref[...]
    act = c * jax.nn.sigmoid(c)
    a_hi, a_lo = _split_bf16(act)
    w_hi, w_lo = _split_bf16(w_ref[0])
    acc = jnp.dot(a_hi, w_hi, preferred_element_type=_F32)
    acc += jnp.dot(a_hi, w_lo, preferred_element_type=_F32)
    acc += jnp.dot(a_lo, w_hi, preferred_element_type=_F32)
    o_ref[0] = acc + b_ref[0]


def _ada_modulation(c_all, w_ada, b_ada):
    n_cols = N_MOD * D_MODEL
    return pl.pallas_call(
        _ada_kernel,
        grid=(DEPTH, n_cols // ADA_COLS),
        in_specs=[
            pl.BlockSpec((MOD_ROWS, D_MODEL), lambda l, j: (0, 0)),
            pl.BlockSpec((1, D_MODEL, ADA_COLS), lambda l, j: (l, 0, j)),
            pl.BlockSpec((1, 1, ADA_COLS), lambda l, j: (l, 0, j)),
        ],
        out_specs=pl.BlockSpec((1, MOD_ROWS, ADA_COLS), lambda l, j: (l, 0, j)),
        out_shape=jax.ShapeDtypeStruct((DEPTH, MOD_ROWS, n_cols), _F32),
        compiler_params=_params(),
        name="ada_modulation",
    )(c_all, w_ada, b_ada.reshape(DEPTH, 1, n_cols))


_T_SPLITS = (ATTN_WIDTH, KV_WIDTH, RET_WIDTH, RET_WIDTH, RET_WIDTH)
_T_WIDTH = sum(_T_SPLITS)
_N_SPLITS = (KV_WIDTH, RET_WIDTH)


def _modulated_norm(x, gain, shift, scale):
    ms = jnp.mean(x * x, axis=-1, keepdims=True)
    return (x * lax.rsqrt(ms + EPS) * gain) * (1.0 + scale) + shift


def _inproj_kernel(x_ref, mod_ref, g_ref, wn_ref, wt_ref, ka_ref, kr_ref, qa_ref, va_ref, qr_ref, vr_ref, gr_ref):
    h = _modulated_norm(x_ref[0], g_ref[...], mod_ref[0, 0:1, :], mod_ref[0, 1:2, :]).astype(_BF16)
    off = 0
    for o_ref, width in zip((ka_ref, kr_ref), _N_SPLITS):
        o_ref[0] = jnp.dot(h, wn_ref[:, off:off + width], preferred_element_type=_F32).astype(_BF16)
        off += width
    off = 0
    for o_ref, width in zip((qa_ref, va_ref, qr_ref, vr_ref, gr_ref), _T_SPLITS):
        p = lax.dot_general(wt_ref[off:off + width, :], h, _NT, preferred_element_type=_F32)
        o_ref[0] = p.astype(_BF16)
        off += width


def _in_projection(x, mod, gain, w_nat, w_t):
    B, S, _ = x.shape
    row = lambda b, i: (b, i, 0)
    col = lambda b, i: (b, 0, i)
    const = lambda b, i: (0, 0)
    return pl.pallas_call(
        _inproj_kernel,
        grid=(B, S // ROW_TILE),
        in_specs=[
            pl.BlockSpec((1, ROW_TILE, D_MODEL), row),
            pl.BlockSpec((1, N_MOD, D_MODEL), lambda b, i: (b, 0, 0)),
            pl.BlockSpec((1, D_MODEL), const),
            pl.BlockSpec((D_MODEL, sum(_N_SPLITS)), const),
            pl.BlockSpec((_T_WIDTH, D_MODEL), const),
        ],
        out_specs=[pl.BlockSpec((1, ROW_TILE, w), row) for w in _N_SPLITS]
        + [pl.BlockSpec((1, w, ROW_TILE), col) for w in _T_SPLITS],
        out_shape=[jax.ShapeDtypeStruct((B, S, w), _BF16) for w in _N_SPLITS]
        + [jax.ShapeDtypeStruct((B, w, S), _BF16) for w in _T_SPLITS],
        compiler_params=_params(),
        name="in_projection",
    )(x, mod, gain, w_nat, w_t)


def _log_decay(decay):
    return jnp.minimum(decay, 0.0) - jnp.log1p(jnp.exp(-jnp.abs(decay)))


def _state_kernel(dec_ref, kf_ref, vf_ref, kb_ref, vb_ref, rf_ref, rb_ref, sf_ref, sb_ref, ktab_ref, ctab_ref):
    j = pl.program_id(1)

    @pl.when(j == 0)
    def _():
        sf_ref[...] = jnp.zeros_like(sf_ref)
        sb_ref[...] = jnp.zeros_like(sb_ref)
        pos = lax.broadcasted_iota(jnp.int32, (BLOCK, 1), 0).astype(_F32)
        for h in range(RET_HEADS):
            cols = slice(h * RET_HEAD_DIM, (h + 1) * RET_HEAD_DIM)
            for d in range(2):
                lg = _log_decay(dec_ref[d:d + 1, cols])
                expo = (BLOCK - 1.0 - pos) if d == 0 else pos
                ktab_ref[d, h] = (RET_HEAD_DIM ** -0.5) * jnp.exp(expo * lg)
                ctab_ref[d, h] = jnp.broadcast_to(jnp.exp(float(BLOCK) * lg), (8, RET_HEAD_DIM))

    for step in range(STATE_CHUNKS):
        for d, (k_ref, v_ref, r_ref, s_ref) in enumerate(
                ((kf_ref, vf_ref, rf_ref, sf_ref), (kb_ref, vb_ref, rb_ref, sb_ref))):
            c = step if d == 0 else STATE_CHUNKS - 1 - step
            rows = slice(c * BLOCK, (c + 1) * BLOCK)
            for h in range(RET_HEADS):
                cols = slice(h * RET_HEAD_DIM, (h + 1) * RET_HEAD_DIM)
                k_dec = (k_ref[0, rows, cols].astype(_F32) * ktab_ref[d, h]).astype(_BF16)
                kv_t = jnp.dot(v_ref[0, cols, rows], k_dec, preferred_element_type=_F32)
                state = s_ref[h]
                r_ref[0, c, h] = state.astype(_BF16)
                s_ref[h] = state * ctab_ref[d, h, 0:1, :] + kv_t


def _retention_states(k_r, v_rt, decays):
    B, S, _ = k_r.shape
    n_steps = S // (BLOCK * STATE_CHUNKS)
    span = BLOCK * STATE_CHUNKS
    st_shape = jax.ShapeDtypeStruct((B, S // BLOCK, RET_HEADS, RET_HEAD_DIM, RET_HEAD_DIM), _BF16)
    st_block = (1, STATE_CHUNKS, RET_HEADS, RET_HEAD_DIM, RET_HEAD_DIM)
    return pl.pallas_call(
        _state_kernel,
        grid=(B, n_steps),
        in_specs=[
            pl.BlockSpec((2, RET_WIDTH), lambda b, j: (0, 0)),
            pl.BlockSpec((1, span, RET_WIDTH), lambda b, j: (b, j, 0)),
            pl.BlockSpec((1, RET_WIDTH, span), lambda b, j: (b, 0, j)),
            pl.BlockSpec((1, span, RET_WIDTH), lambda b, j: (b, n_steps - 1 - j, 0)),
            pl.BlockSpec((1, RET_WIDTH, span), lambda b, j: (b, 0, n_steps - 1 - j)),
        ],
        out_specs=[
            pl.BlockSpec(st_block, lambda b, j: (b, j, 0, 0, 0)),
            pl.BlockSpec(st_block, lambda b, j: (b, n_steps - 1 - j, 0, 0, 0)),
        ],
        out_shape=[st_shape, st_shape],
        scratch_shapes=[
            pltpu.VMEM((RET_HEADS, RET_HEAD_DIM, RET_HEAD_DIM), _F32),
            pltpu.VMEM((RET_HEADS, RET_HEAD_DIM, RET_HEAD_DIM), _F32),
            pltpu.VMEM((2, RET_HEADS, BLOCK, RET_HEAD_DIM), _F32),
            pltpu.VMEM((2, RET_HEADS, 8, RET_HEAD_DIM), _F32),
        ],
        compiler_params=_params(),
        name="retention_states",
    )(decays, k_r, v_rt, k_r, v_rt)


def _mix_kernel(sink_ref, dec_ref, qa_ref, kp_ref, kc_ref, kn_ref, vp_ref, vc_ref, vn_ref,
                qr_ref, kr_ref, vr_ref, gr_ref, rf_ref, rb_ref, o_ref,
                bias_ref, dcomb_ref, qdec_ref, s_ref, p_ref, *, n_blocks):
    n = pl.program_id(1)

    @pl.when(n == 0)
    def _():
        jj = lax.broadcasted_iota(jnp.int32, (BAND, BLOCK), 0)
        ii = lax.broadcasted_iota(jnp.int32, (BAND, BLOCK), 1)
        dist = jnp.abs(jj - BLOCK - ii)
        distf = dist.astype(_F32)
        has_prev, has_next = jj >= BLOCK, jj < 2 * BLOCK
        for hh in range(ATTN_HEADS):
            slope = 2.0 ** (-8.0 * (hh + 1.0) / ATTN_HEADS)
            table = jnp.where(dist <= WINDOW, (-slope * LOG2_E) * distf, NEG_INF)
            bias_ref[0, hh] = table
            bias_ref[1, hh] = jnp.where(has_prev, table, NEG_INF)
            bias_ref[2, hh] = jnp.where(has_next, table, NEG_INF)
        rj = lax.broadcasted_iota(jnp.int32, (BLOCK, BLOCK), 0)
        ri = lax.broadcasted_iota(jnp.int32, (BLOCK, BLOCK), 1)
        diff = (ri - rj).astype(_F32)
        posq = lax.broadcasted_iota(jnp.int32, (8, BLOCK), 1).astype(_F32)
        for h in range(RET_HEADS):
            cols = slice(h * RET_HEAD_DIM, (h + 1) * RET_HEAD_DIM)
            lg_f = _log_decay(dec_ref[0:1, cols])
            lg_b = _log_decay(dec_ref[1:2, cols])
            d_f = jnp.exp(jnp.maximum(diff, 0.0) * lg_f)
            d_b = jnp.exp(jnp.maximum(-diff, 0.0) * lg_b)
            dcomb_ref[h] = jnp.where(diff >= 0, d_f, d_b) * (RET_HEAD_DIM ** -0.5)
            qdec_ref[0, h] = jnp.exp((posq + 1.0) * lg_f)
            qdec_ref[1, h] = jnp.exp((float(BLOCK) - posq) * lg_b)

    kband = jnp.concatenate([kp_ref[0], kc_ref[0], kn_ref[0]], axis=0)
    vband = jnp.concatenate([vp_ref[0], vc_ref[0], vn_ref[0]], axis=1)
    zeros = jnp.zeros((ATTN_HEAD_DIM, ATTN_GROUP * BLOCK), _BF16)
    q_rows = []
    for kh in range(ATTN_KV_HEADS):
        tiles = [qa_ref[0, (kh * ATTN_GROUP + g) * ATTN_HEAD_DIM:(kh * ATTN_GROUP + g + 1) * ATTN_HEAD_DIM, :]
                 for g in range(ATTN_GROUP)]
        mine = jnp.concatenate(tiles, axis=1)
        q_rows.append(jnp.concatenate([mine, zeros] if kh == 0 else [zeros, mine], axis=1))
    q_bd = jnp.concatenate(q_rows, axis=0)
    s_ref[...] = jnp.dot(kband, q_bd, preferred_element_type=_F32)

    ret_parts = []
    for h in range(RET_HEADS):
        rows = slice(h * RET_HEAD_DIM, (h + 1) * RET_HEAD_DIM)
        q_t = qr_ref[0, rows, :]
        s_t = jnp.dot(kr_ref[0, :, rows], q_t, preferred_element_type=_F32)
        cross = jnp.dot(rf_ref[0, 0, h], q_t, preferred_element_type=_F32) * qdec_ref[0, h, 0:1, :]
        cross = cross + jnp.dot(rb_ref[0, 0, h], q_t, preferred_element_type=_F32) * qdec_ref[1, h, 0:1, :]
        ret_parts.append(((s_t * dcomb_ref[h]).astype(_BF16), cross))

    def finish_retention(h):
        rows = slice(h * RET_HEAD_DIM, (h + 1) * RET_HEAD_DIM)
        a_t, cross = ret_parts[h]
        y = jnp.dot(vr_ref[0, rows, :], a_t, preferred_element_type=_F32) + cross
        yc = y - jnp.mean(y, axis=0, keepdims=True)
        yn = yc * lax.rsqrt(jnp.mean(yc * yc, axis=0, keepdims=True) + EPS)
        gate = gr_ref[0, rows, :].astype(_F32)
        lanes = slice(ATTN_WIDTH + h * RET_HEAD_DIM, ATTN_WIDTH + (h + 1) * RET_HEAD_DIM)
        o_ref[0, :, lanes] = (gate * jax.nn.sigmoid(gate) * yn).T.astype(_BF16)

    variant = jnp.where(n == 0, 1, jnp.where(n == n_blocks - 1, 2, 0))
    inv_rows = []

    def softmax_head(hh):
        lanes = slice(hh * BLOCK, (hh + 1) * BLOCK)
        st = s_ref[:, lanes] + bias_ref[variant, hh]
        sink = sink_ref[hh] * LOG2_E
        m = jnp.maximum(jnp.max(st, axis=0, keepdims=True), sink)
        p = jnp.exp2(st - m)
        denom = jnp.sum(p, axis=0, keepdims=True) + jnp.exp2(sink - m)
        p_ref[:, lanes] = p.astype(_BF16)
        inv_rows.append(1.0 / denom)

    def attend(kh):
        lanes = slice(kh * ATTN_GROUP * BLOCK, (kh + 1) * ATTN_GROUP * BLOCK)
        v_t = vband[kh * ATTN_HEAD_DIM:(kh + 1) * ATTN_HEAD_DIM, :]
        o_t = jnp.dot(v_t, p_ref[:, lanes], preferred_element_type=_F32)
        for pair in range(ATTN_GROUP // 2):
            heads = (kh * ATTN_GROUP + 2 * pair, kh * ATTN_GROUP + 2 * pair + 1)
            both = jnp.concatenate(
                [o_t[:, (2 * pair + g) * BLOCK:(2 * pair + g + 1) * BLOCK] * inv_rows[heads[g]] for g in range(2)],
                axis=0)
            o_ref[0, :, heads[0] * ATTN_HEAD_DIM:(heads[1] + 1) * ATTN_HEAD_DIM] = both.T.astype(_BF16)

    for hh in range(ATTN_HEADS):
        softmax_head(hh)
        if hh % 2 == 1:
            finish_retention(hh // 2)
        if hh == ATTN_GROUP - 1:
            attend(0)
    attend(1)


def _token_mixing(qa_t, ka, va_t, qr_t, kr, vr_t, gr_t, rf, rb, sink, decays):
    B, S, _ = ka.shape
    N = S // BLOCK
    lo = lambda n: jnp.maximum(n - 1, 0)
    hi = lambda n: jnp.minimum(n + 1, N - 1)
    tok = lambda w, f: pl.BlockSpec((1, BLOCK, w), lambda b, n: (b, f(n), 0))
    feat = lambda w, f: pl.BlockSpec((1, w, BLOCK), lambda b, n: (b, 0, f(n)))
    same = lambda n: n
    st_spec = pl.BlockSpec((1, 1, RET_HEADS, RET_HEAD_DIM, RET_HEAD_DIM), lambda b, n: (b, n, 0, 0, 0))
    return pl.pallas_call(
        functools.partial(_mix_kernel, n_blocks=N),
        grid=(B, N),
        in_specs=[
            pl.BlockSpec(memory_space=pltpu.SMEM),
            pl.BlockSpec((2, RET_WIDTH), lambda b, n: (0, 0)),
            feat(ATTN_WIDTH, same),
            tok(KV_WIDTH, lo), tok(KV_WIDTH, same), tok(KV_WIDTH, hi),
            feat(KV_WIDTH, lo), feat(KV_WIDTH, same), feat(KV_WIDTH, hi),
            feat(RET_WIDTH, same), tok(RET_WIDTH, same), feat(RET_WIDTH, same), feat(RET_WIDTH, same),
            st_spec, st_spec,
        ],
        out_specs=pl.BlockSpec((1, BLOCK, D_MODEL), lambda b, n: (b, n, 0)),
        out_shape=jax.ShapeDtypeStruct((B, S, D_MODEL), _BF16),
        scratch_shapes=[
            pltpu.VMEM((3, ATTN_HEADS, BAND, BLOCK), _F32),
            pltpu.VMEM((RET_HEADS, BLOCK, BLOCK), _F32),
            pltpu.VMEM((2, RET_HEADS, 8, BLOCK), _F32),
            pltpu.VMEM((BAND, ATTN_HEADS * BLOCK), _F32),
            pltpu.VMEM((BAND, ATTN_HEADS * BLOCK), _BF16),
        ],
        compiler_params=_params(),
        name="token_mixing",
    )(sink, decays, qa_t, ka, ka, ka, va_t, va_t, va_t, qr_t, kr, vr_t, gr_t, rf, rb)


def _mlp_kernel(x_ref, mix_ref, mod_ref, g2_ref, gf_ref, wo_ref, w1_ref, w2_ref, o_ref, *, final):
    x1 = x_ref[0] + mod_ref[0, 2:3, :] * jnp.dot(mix_ref[0], wo_ref[...], preferred_element_type=_F32)
    h = _modulated_norm(x1, g2_ref[...], mod_ref[0, 3:4, :], mod_ref[0, 4:5, :]).astype(_BF16)
    acc = jnp.zeros((ROW_TILE, D_MODEL), _F32)
    for c in range(D_FF // FF_CHUNK):
        cols = slice(c * FF_CHUNK, (c + 1) * FF_CHUNK)
        a = jnp.dot(h, w1_ref[:, cols], preferred_element_type=_F32)
        a = jnp.square(jnp.maximum(a, 0.0)).astype(_BF16)
        acc = acc + jnp.dot(a, w2_ref[cols, :], preferred_element_type=_F32)
    x2 = x1 + mod_ref[0, 5:6, :] * acc
    if final:
        ms = jnp.mean(x2 * x2, axis=-1, keepdims=True)
        x2 = x2 * lax.rsqrt(ms + EPS) * gf_ref[...]
    o_ref[0] = x2


def _channel_mixing(x, mix, mod, gain2, gain_final, w_out, w_mlp1, w_mlp2, final):
    B, S, _ = x.shape
    row = lambda b, i: (b, i, 0)
    const = lambda b, i: (0, 0)
    resident = dict(pipeline_mode=pl.Buffered(1))
    return pl.pallas_call(
        functools.partial(_mlp_kernel, final=final),
        grid=(B, S // ROW_TILE),
        in_specs=[
            pl.BlockSpec((1, ROW_TILE, D_MODEL), row),
            pl.BlockSpec((1, ROW_TILE, D_MODEL), row),
            pl.BlockSpec((1, N_MOD, D_MODEL), lambda b, i: (b, 0, 0)),
            pl.BlockSpec((1, D_MODEL), const),
            pl.BlockSpec((1, D_MODEL), const),
            pl.BlockSpec((D_MODEL, D_MODEL), const, **resident),
            pl.BlockSpec((D_MODEL, D_FF), const, **resident),
            pl.BlockSpec((D_FF, D_MODEL), const, **resident),
        ],
        out_specs=pl.BlockSpec((1, ROW_TILE, D_MODEL), row),
        out_shape=jax.ShapeDtypeStruct((B, S, D_MODEL), _F32),
        compiler_params=_params(),
        name="channel_mixing",
    )(x, mix, mod, gain2, gain_final, w_out, w_mlp1, w_mlp2)


def _run_trunk(x, mod_rows, norm1_g, w_nat, w_t, attn_sink, decays, w_out, norm2_g, w_mlp1, w_mlp2, final_g):
    B = x.shape[0]
    for l in range(DEPTH):
        mod = mod_rows[l].reshape(B, N_MOD, D_MODEL)
        ka, kr, qa_t, va_t, qr_t, vr_t, gr_t = _in_projection(x, mod, norm1_g[l:l + 1], w_nat[l], w_t[l])
        rf, rb = _retention_states(kr, vr_t, decays[l])
        mix = _token_mixing(qa_t, ka, va_t, qr_t, kr, vr_t, gr_t, rf, rb, attn_sink[l], decays[l])
        x = _channel_mixing(x, mix, mod, norm2_g[l:l + 1], final_g, w_out[l], w_mlp1[l], w_mlp2[l],
                            final=(l == DEPTH - 1))
    return x


def kernel(x_prompt, x_sample, c_prompt, c_sample, w_ada, b_ada, norm1_g, w_in, attn_sink,
           ret_decay_fwd, ret_decay_bwd, w_out, norm2_g, w_mlp1, w_mlp2, final_g):
    bp, bs = x_prompt.shape[0], x_sample.shape[0]
    c_all = jnp.concatenate(
        [c_prompt, c_sample, jnp.zeros((MOD_ROWS - bp - bs, D_MODEL), _F32)], axis=0)
    mod_all = _ada_modulation(c_all, w_ada, b_ada)
    decays = jnp.stack([jnp.repeat(ret_decay_fwd, RET_HEAD_DIM, axis=-1),
                        jnp.repeat(ret_decay_bwd, RET_HEAD_DIM, axis=-1)], axis=1)
    w_nat = jnp.concatenate([w_in[:, :, OFF_KA:OFF_VA], w_in[:, :, OFF_KR:OFF_VR]], axis=-1).astype(_BF16)
    w_t = jnp.concatenate(
        [w_in[:, :, OFF_QA:OFF_KA] * (ATTN_HEAD_DIM ** -0.5 * LOG2_E), w_in[:, :, OFF_VA:OFF_QR],
         w_in[:, :, OFF_QR:OFF_KR], w_in[:, :, OFF_VR:OFF_GR], w_in[:, :, OFF_GR:IN_WIDTH]],
        axis=-1).astype(_BF16).transpose(0, 2, 1)
    shared = (norm1_g, w_nat, w_t, attn_sink, decays, w_out.astype(_BF16), norm2_g,
              w_mlp1.astype(_BF16), w_mlp2.astype(_BF16), final_g.reshape(1, D_MODEL))
    y_prompt = _run_trunk(x_prompt, mod_all[:, :bp], *shared)
    y_sample = _run_trunk(x_sample, mod_all[:, bp:bp + bs], *shared)
    return (y_prompt, y_sample)
```

```python
import functools

import jax
import jax.numpy as jnp
from jax import lax
from jax.experimental import pallas as pl
from jax.experimental.pallas import tpu as pltpu

D_MODEL = 1024
DEPTH = 4
ATTN_HEAD_DIM = 64
ATTN_WIDTH = 512
ATTN_HEADS = 8
ATTN_KV_HEADS = 2
ATTN_GROUP = 4
KV_WIDTH = 128
WINDOW = 128
BLOCK = 128
BAND = 3 * BLOCK
RET_WIDTH = 512
RET_HEAD_DIM = 128
RET_HEADS = 4
D_FF = 4 * D_MODEL
N_MOD = 6
EPS = 1e-6
NEG_INF = -1e30
LOG2_E = 1.4426950408889634
OFF_QA, OFF_KA, OFF_VA, OFF_QR, OFF_KR, OFF_VR, OFF_GR, IN_WIDTH = 0, 512, 640, 768, 1280, 1792, 2304, 2816

F_QA, F_QR, F_VR, F_GR, F_VA, F_ROWS = 0, 512, 1024, 1536, 2048, 2176
K_RET, K_ATT, K_COLS = 0, 512, 640

ROW_TILE = 512
FF_CHUNK = 1024
ADA_COLS = 1536
MOD_ROWS = 16
MIX_BLOCKS = 2
VMEM_LIMIT = 56 * 1024 * 1024

_BF16 = jnp.bfloat16
_F32 = jnp.float32
_BF16_ROWS = 16
_NT = (((1,), (1,)), ((), ()))


def _split_bf16(a):
    hi = a.astype(_BF16)
    lo = (a - hi.astype(_F32)).astype(_BF16)
    return hi, lo


def _params(n_axes):
    return pltpu.CompilerParams(
        dimension_semantics=("arbitrary",) * n_axes, vmem_limit_bytes=VMEM_LIMIT)


def _ada_kernel(c_ref, w_ref, b_ref, o_ref):
    c = c_ref[...]
    act = c * jax.nn.sigmoid(c)
    a_hi, a_lo = _split_bf16(act)
    w_hi, w_lo = _split_bf16(w_ref[0])
    acc = jnp.dot(a_hi, w_hi, preferred_element_type=_F32)
    acc += jnp.dot(a_hi, w_lo, preferred_element_type=_F32)
    acc += jnp.dot(a_lo, w_hi, preferred_element_type=_F32)
    o_ref[0] = acc + b_ref[0]


def _ada_modulation(c_all, w_ada, b_ada):
    n_cols = N_MOD * D_MODEL
    return pl.pallas_call(
        _ada_kernel,
        grid=(DEPTH, n_cols // ADA_COLS),
        in_specs=[
            pl.BlockSpec((MOD_ROWS, D_MODEL), lambda l, j: (0, 0)),
            pl.BlockSpec((1, D_MODEL, ADA_COLS), lambda l, j: (l, 0, j)),
            pl.BlockSpec((1, 1, ADA_COLS), lambda l, j: (l, 0, j)),
        ],
        out_specs=pl.BlockSpec((1, MOD_ROWS, ADA_COLS), lambda l, j: (l, 0, j)),
        out_shape=jax.ShapeDtypeStruct((DEPTH, MOD_ROWS, n_cols), _F32),
        compiler_params=_params(2),
        name="ada_modulation",
    )(c_all, w_ada, b_ada.reshape(DEPTH, 1, n_cols))


_F_GROUPS = ((F_QA, ATTN_WIDTH), (F_QR, RET_WIDTH), (F_VR, RET_WIDTH), (F_GR, RET_WIDTH), (F_VA, KV_WIDTH))


def _modulated_norm(x, gain, shift, scale):
    ms = jnp.mean(x * x, axis=-1, keepdims=True)
    return (x * lax.rsqrt(ms + EPS) * gain) * (1.0 + scale) + shift


def _inproj_kernel(x_ref, mod_ref, g_ref, wn_ref, wt_ref, k_ref, f_ref):
    h = _modulated_norm(x_ref[0], g_ref[...], mod_ref[0, 0:1, :], mod_ref[0, 1:2, :]).astype(_BF16)
    k_ref[0] = jnp.dot(h, wn_ref[...], preferred_element_type=_F32).astype(_BF16)
    for off, width in _F_GROUPS:
        p = lax.dot_general(wt_ref[off:off + width, :], h, _NT, preferred_element_type=_F32)
        for c in range(ROW_TILE // BLOCK):
            f_ref[0, c, off:off + width, :] = p[:, c * BLOCK:(c + 1) * BLOCK].astype(_BF16)


def _in_projection(x, mod, gain, w_nat, w_t):
    B, S, _ = x.shape
    blocks = ROW_TILE // BLOCK
    const = lambda b, i: (0, 0)
    return pl.pallas_call(
        _inproj_kernel,
        grid=(B, S // ROW_TILE),
        in_specs=[
            pl.BlockSpec((1, ROW_TILE, D_MODEL), lambda b, i: (b, i, 0)),
            pl.BlockSpec((1, N_MOD, D_MODEL), lambda b, i: (b, 0, 0)),
            pl.BlockSpec((1, D_MODEL), const),
            pl.BlockSpec((D_MODEL, K_COLS), const),
            pl.BlockSpec((F_ROWS, D_MODEL), const),
        ],
        out_specs=[
            pl.BlockSpec((1, ROW_TILE, K_COLS), lambda b, i: (b, i, 0)),
            pl.BlockSpec((1, blocks, F_ROWS, BLOCK), lambda b, i: (b, i, 0, 0)),
        ],
        out_shape=[
            jax.ShapeDtypeStruct((B, S, K_COLS), _BF16),
            jax.ShapeDtypeStruct((B, S // BLOCK, F_ROWS, BLOCK), _BF16),
        ],
        compiler_params=_params(2),
        name="in_projection",
    )(x, mod, gain, w_nat, w_t)


def _log_decay(decay):
    return jnp.minimum(decay, 0.0) - jnp.log1p(jnp.exp(-jnp.abs(decay)))


def _state_kernel(dec_ref, k_ref, v_ref, rf_ref, rb_ref, s_ref, ktab_ref, ctab_ref, *, n_chunks):
    s_ref[...] = jnp.zeros_like(s_ref)
    pos = lax.broadcasted_iota(jnp.int32, (BLOCK, 1), 0).astype(_F32)
    for h in range(RET_HEADS):
        cols = slice(h * RET_HEAD_DIM, (h + 1) * RET_HEAD_DIM)
        for d in range(2):
            lg = _log_decay(dec_ref[d:d + 1, cols])
            expo = (BLOCK - 1.0 - pos) if d == 0 else pos
            ktab_ref[d, h] = (RET_HEAD_DIM ** -0.5) * jnp.exp(expo * lg)
            ctab_ref[d, h] = jnp.broadcast_to(jnp.exp(float(BLOCK) * lg), (8, RET_HEAD_DIM))

    def body(i, carry):
        for d, r_ref in enumerate((rf_ref, rb_ref)):
            c = i if d == 0 else n_chunks - 1 - i
            row0 = pl.multiple_of(c * BLOCK, BLOCK)
            for h in range(RET_HEADS):
                cols = slice(h * RET_HEAD_DIM, (h + 1) * RET_HEAD_DIM)
                k_dec = (k_ref[0, pl.ds(row0, BLOCK), cols].astype(_F32) * ktab_ref[d, h]).astype(_BF16)
                kv_t = jnp.dot(v_ref[0, c, cols, :], k_dec, preferred_element_type=_F32)
                state = s_ref[d, h]
                r_ref[0, c, h] = state.astype(_BF16)
                s_ref[d, h] = state * ctab_ref[d, h, 0:1, :] + kv_t
        return carry

    lax.fori_loop(0, n_chunks, body, 0)


def _retention_states(k_nat, feat, decays):
    B, S, _ = k_nat.shape
    N = S // BLOCK
    st_shape = jax.ShapeDtypeStruct((B, N, RET_HEADS, RET_HEAD_DIM, RET_HEAD_DIM), _BF16)
    st_spec = pl.BlockSpec((1, N, RET_HEADS, RET_HEAD_DIM, RET_HEAD_DIM), lambda b: (b, 0, 0, 0, 0))
    return pl.pallas_call(
        functools.partial(_state_kernel, n_chunks=N),
        grid=(B,),
        in_specs=[
            pl.BlockSpec((2, RET_WIDTH), lambda b: (0, 0)),
            pl.BlockSpec((1, S, RET_WIDTH), lambda b: (b, 0, K_RET // RET_WIDTH)),
            pl.BlockSpec((1, N, RET_WIDTH, BLOCK), lambda b: (b, 0, F_VR // RET_WIDTH, 0)),
        ],
        out_specs=[st_spec, st_spec],
        out_shape=[st_shape, st_shape],
        scratch_shapes=[
            pltpu.VMEM((2, RET_HEADS, RET_HEAD_DIM, RET_HEAD_DIM), _F32),
            pltpu.VMEM((2, RET_HEADS, BLOCK, RET_HEAD_DIM), _F32),
            pltpu.VMEM((2, RET_HEADS, 8, RET_HEAD_DIM), _F32),
        ],
        compiler_params=_params(1),
        name="retention_states",
    )(decays, k_nat, feat)


def _mix_block(sub, variant, sink_ref, feat_ref, k_ref, k_blocks, v_blocks, rf_ref, rb_ref, o_ref,
               bias_ref, dcomb_ref, qdec_ref, s_ref, p_ref):
    tok = slice(sub * BLOCK, (sub + 1) * BLOCK)
    kband = jnp.concatenate(k_blocks[sub:sub + 3], axis=0)
    vband = jnp.concatenate(v_blocks[sub:sub + 3], axis=1)

    zeros = jnp.zeros((ATTN_HEAD_DIM, ATTN_GROUP * BLOCK), _BF16)
    q_rows = []
    for kh in range(ATTN_KV_HEADS):
        tiles = [feat_ref[0, sub, F_QA + hh * ATTN_HEAD_DIM:F_QA + (hh + 1) * ATTN_HEAD_DIM, :]
                 for hh in range(kh * ATTN_GROUP, (kh + 1) * ATTN_GROUP)]
        mine = jnp.concatenate(tiles, axis=1)
        q_rows.append(jnp.concatenate([mine, zeros] if kh == 0 else [zeros, mine], axis=1))
    q_bd = jnp.concatenate(q_rows, axis=0)
    s_ref[sub] = jnp.dot(kband, q_bd, preferred_element_type=_F32)

    ret_parts = []
    for h in range(RET_HEADS):
        rows = slice(h * RET_HEAD_DIM, (h + 1) * RET_HEAD_DIM)
        q_t = feat_ref[0, sub, F_QR + h * RET_HEAD_DIM:F_QR + (h + 1) * RET_HEAD_DIM, :]
        s_t = jnp.dot(k_ref[0, tok, rows], q_t, preferred_element_type=_F32)
        cross = jnp.dot(rf_ref[0, sub, h], q_t, preferred_element_type=_F32) * qdec_ref[0, h, 0:1, :]
        cross = cross + jnp.dot(rb_ref[0, sub, h], q_t, preferred_element_type=_F32) * qdec_ref[1, h, 0:1, :]
        ret_parts.append(((s_t * dcomb_ref[h]).astype(_BF16), cross))

    def finish_retention(h):
        a_t, cross = ret_parts[h]
        v_t = feat_ref[0, sub, F_VR + h * RET_HEAD_DIM:F_VR + (h + 1) * RET_HEAD_DIM, :]
        y = jnp.dot(v_t, a_t, preferred_element_type=_F32) + cross
        yc = y - jnp.mean(y, axis=0, keepdims=True)
        yn = yc * lax.rsqrt(jnp.mean(yc * yc, axis=0, keepdims=True) + EPS)
        gate = feat_ref[0, sub, F_GR + h * RET_HEAD_DIM:F_GR + (h + 1) * RET_HEAD_DIM, :].astype(_F32)
        lanes = slice(ATTN_WIDTH + h * RET_HEAD_DIM, ATTN_WIDTH + (h + 1) * RET_HEAD_DIM)
        o_ref[0, tok, lanes] = (gate * jax.nn.sigmoid(gate) * yn).T.astype(_BF16)

    sink_terms = []
    ones = jnp.ones((_BF16_ROWS, BAND), _BF16)

    def softmax_head(hh):
        lanes = slice(hh * BLOCK, (hh + 1) * BLOCK)
        st = s_ref[sub, :, lanes] + bias_ref[variant, hh]
        sink = sink_ref[hh] * LOG2_E
        m = jnp.maximum(jnp.max(st, axis=0, keepdims=True), sink)
        p_ref[sub, :, lanes] = jnp.exp2(st - m).astype(_BF16)
        sink_terms.append(jnp.exp2(sink - m))

    def attend(kh):
        lanes = slice(kh * ATTN_GROUP * BLOCK, (kh + 1) * ATTN_GROUP * BLOCK)
        v_t = jnp.concatenate([vband[kh * ATTN_HEAD_DIM:(kh + 1) * ATTN_HEAD_DIM, :], ones], axis=0)
        o_t = jnp.dot(v_t, p_ref[sub, :, lanes], preferred_element_type=_F32)
        for pair in range(ATTN_GROUP // 2):
            scaled = []
            for g in (2 * pair, 2 * pair + 1):
                q_lanes = slice(g * BLOCK, (g + 1) * BLOCK)
                denom = o_t[ATTN_HEAD_DIM:ATTN_HEAD_DIM + 1, q_lanes] + sink_terms[kh * ATTN_GROUP + g]
                scaled.append(o_t[:ATTN_HEAD_DIM, q_lanes] * (1.0 / denom))
            both = jnp.concatenate(scaled, axis=0)
            first = (kh * ATTN_GROUP + 2 * pair) * ATTN_HEAD_DIM
            o_ref[0, tok, first:first + 2 * ATTN_HEAD_DIM] = both.T.astype(_BF16)

    for hh in range(ATTN_HEADS):
        softmax_head(hh)
        if hh % 2 == 1:
            finish_retention(hh // 2)
        if hh == ATTN_GROUP - 1:
            attend(0)
    attend(1)


def _mix_kernel(sink_ref, dec_ref, feat_ref, k_ref, kp_ref, kn_ref, vp_ref, vn_ref, rf_ref, rb_ref, o_ref,
                bias_ref, dcomb_ref, qdec_ref, s_ref, p_ref, *, n_steps):
    step = pl.program_id(1)

    @pl.when(step == 0)
    def _():
        jj = lax.broadcasted_iota(jnp.int32, (BAND, BLOCK), 0)
        ii = lax.broadcasted_iota(jnp.int32, (BAND, BLOCK), 1)
        dist = jnp.abs(jj - BLOCK - ii)
        distf = dist.astype(_F32)
        has_prev, has_next = jj >= BLOCK, jj < 2 * BLOCK
        for hh in range(ATTN_HEADS):
            slope = 2.0 ** (-8.0 * (hh + 1.0) / ATTN_HEADS)
            table = jnp.where(dist <= WINDOW, (-slope * LOG2_E) * distf, NEG_INF)
            bias_ref[0, hh] = table
            bias_ref[1, hh] = jnp.where(has_prev, table, NEG_INF)
            bias_ref[2, hh] = jnp.where(has_next, table, NEG_INF)
        rj = lax.broadcasted_iota(jnp.int32, (BLOCK, BLOCK), 0)
        ri = lax.broadcasted_iota(jnp.int32, (BLOCK, BLOCK), 1)
        diff = (ri - rj).astype(_F32)
        posq = lax.broadcasted_iota(jnp.int32, (8, BLOCK), 1).astype(_F32)
        for h in range(RET_HEADS):
            cols = slice(h * RET_HEAD_DIM, (h + 1) * RET_HEAD_DIM)
            lg_f = _log_decay(dec_ref[0:1, cols])
            lg_b = _log_decay(dec_ref[1:2, cols])
            d_f = jnp.exp(jnp.maximum(diff, 0.0) * lg_f)
            d_b = jnp.exp(jnp.maximum(-diff, 0.0) * lg_b)
            dcomb_ref[h] = jnp.where(diff >= 0, d_f, d_b) * (RET_HEAD_DIM ** -0.5)
            qdec_ref[0, h] = jnp.exp((posq + 1.0) * lg_f)
            qdec_ref[1, h] = jnp.exp((float(BLOCK) - posq) * lg_b)

    k_blocks = ([kp_ref[0]]
                + [k_ref[0, i * BLOCK:(i + 1) * BLOCK, K_ATT:K_ATT + KV_WIDTH] for i in range(MIX_BLOCKS)]
                + [kn_ref[0]])
    v_blocks = ([vp_ref[0, 0]]
                + [feat_ref[0, i, F_VA:F_VA + KV_WIDTH, :] for i in range(MIX_BLOCKS)]
                + [vn_ref[0, 0]])
    for sub in range(MIX_BLOCKS):
        variant = 0
        if sub == 0:
            variant = jnp.where(step == 0, 1, variant)
        if sub == MIX_BLOCKS - 1:
            variant = jnp.where(step == n_steps - 1, 2, variant)
        _mix_block(sub, variant, sink_ref, feat_ref, k_ref, k_blocks, v_blocks, rf_ref, rb_ref, o_ref,
                   bias_ref, dcomb_ref, qdec_ref, s_ref, p_ref)


def _token_mixing(k_nat, feat, rf, rb, sink, decays):
    B, S, _ = k_nat.shape
    N = S // BLOCK
    n_steps = N // MIX_BLOCKS
    assert n_steps * MIX_BLOCKS == N and N >= 2
    prev_blk = lambda i: jnp.maximum(i * MIX_BLOCKS - 1, 0)
    next_blk = lambda i: jnp.minimum((i + 1) * MIX_BLOCKS, N - 1)
    k_col, v_row = K_ATT // KV_WIDTH, F_VA // KV_WIDTH
    st_spec = pl.BlockSpec((1, MIX_BLOCKS, RET_HEADS, RET_HEAD_DIM, RET_HEAD_DIM), lambda b, i: (b, i, 0, 0, 0))
    return pl.pallas_call(
        functools.partial(_mix_kernel, n_steps=n_steps),
        grid=(B, n_steps),
        in_specs=[
            pl.BlockSpec(memory_space=pltpu.SMEM),
            pl.BlockSpec((2, RET_WIDTH), lambda b, i: (0, 0)),
            pl.BlockSpec((1, MIX_BLOCKS, F_ROWS, BLOCK), lambda b, i: (b, i, 0, 0)),
            pl.BlockSpec((1, MIX_BLOCKS * BLOCK, K_COLS), lambda b, i: (b, i, 0)),
            pl.BlockSpec((1, BLOCK, KV_WIDTH), lambda b, i: (b, prev_blk(i), k_col)),
            pl.BlockSpec((1, BLOCK, KV_WIDTH), lambda b, i: (b, next_blk(i), k_col)),
            pl.BlockSpec((1, 1, KV_WIDTH, BLOCK), lambda b, i: (b, prev_blk(i), v_row, 0)),
            pl.BlockSpec((1, 1, KV_WIDTH, BLOCK), lambda b, i: (b, next_blk(i), v_row, 0)),
            st_spec, st_spec,
        ],
        out_specs=pl.BlockSpec((1, MIX_BLOCKS * BLOCK, D_MODEL), lambda b, i: (b, i, 0)),
        out_shape=jax.ShapeDtypeStruct((B, S, D_MODEL), _BF16),
        scratch_shapes=[
            pltpu.VMEM((3, ATTN_HEADS, BAND, BLOCK), _F32),
            pltpu.VMEM((RET_HEADS, BLOCK, BLOCK), _F32),
            pltpu.VMEM((2, RET_HEADS, 8, BLOCK), _F32),
            pltpu.VMEM((MIX_BLOCKS, BAND, ATTN_HEADS * BLOCK), _F32),
            pltpu.VMEM((MIX_BLOCKS, BAND, ATTN_HEADS * BLOCK), _BF16),
        ],
        compiler_params=_params(2),
        name="token_mixing",
    )(sink, decays, feat, k_nat, k_nat, k_nat, feat, feat, rf, rb)


def _mlp_kernel(x_ref, mix_ref, mod_ref, g2_ref, gf_ref, wo_ref, w1_ref, w2_ref, o_ref, *, final):
    x1 = x_ref[0] + mod_ref[0, 2:3, :] * jnp.dot(mix_ref[0], wo_ref[...], preferred_element_type=_F32)
    h = _modulated_norm(x1, g2_ref[...], mod_ref[0, 3:4, :], mod_ref[0, 4:5, :]).astype(_BF16)
    acc = jnp.zeros((ROW_TILE, D_MODEL), _F32)
    for c in range(D_FF // FF_CHUNK):
        cols = slice(c * FF_CHUNK, (c + 1) * FF_CHUNK)
        a = jnp.dot(h, w1_ref[:, cols], preferred_element_type=_F32)
        a = jnp.square(jnp.maximum(a, 0.0)).astype(_BF16)
        acc = acc + jnp.dot(a, w2_ref[cols, :], preferred_element_type=_F32)
    x2 = x1 + mod_ref[0, 5:6, :] * acc
    if final:
        ms = jnp.mean(x2 * x2, axis=-1, keepdims=True)
        x2 = x2 * lax.rsqrt(ms + EPS) * gf_ref[...]
    o_ref[0] = x2


def _channel_mixing(x, mix, mod, gain2, gain_final, w_out, w_mlp1, w_mlp2, final):
    B, S, _ = x.shape
    row = lambda b, i: (b, i, 0)
    const = lambda b, i: (0, 0)
    resident = dict(pipeline_mode=pl.Buffered(1))
    return pl.pallas_call(
        functools.partial(_mlp_kernel, final=final),
        grid=(B, S // ROW_TILE),
        in_specs=[
            pl.BlockSpec((1, ROW_TILE, D_MODEL), row),
            pl.BlockSpec((1, ROW_TILE, D_MODEL), row),
            pl.BlockSpec((1, N_MOD, D_MODEL), lambda b, i: (b, 0, 0)),
            pl.BlockSpec((1, D_MODEL), const),
            pl.BlockSpec((1, D_MODEL), const),
            pl.BlockSpec((D_MODEL, D_MODEL), const, **resident),
            pl.BlockSpec((D_MODEL, D_FF), const, **resident),
            pl.BlockSpec((D_FF, D_MODEL), const, **resident),
        ],
        out_specs=pl.BlockSpec((1, ROW_TILE, D_MODEL), row),
        out_shape=jax.ShapeDtypeStruct((B, S, D_MODEL), _F32),
        compiler_params=_params(2),
        name="channel_mixing",
    )(x, mix, mod, gain2, gain_final, w_out, w_mlp1, w_mlp2)


def _run_trunk(x, mod_rows, norm1_g, w_nat, w_t, attn_sink, decays, w_out, norm2_g, w_mlp1, w_mlp2, final_g):
    B = x.shape[0]
    for l in range(DEPTH):
        mod = mod_rows[l].reshape(B, N_MOD, D_MODEL)
        k_nat, feat = _in_projection(x, mod, norm1_g[l:l + 1], w_nat[l], w_t[l])
        rf, rb = _retention_states(k_nat, feat, decays[l])
        mix = _token_mixing(k_nat, feat, rf, rb, attn_sink[l], decays[l])
        x = _channel_mixing(x, mix, mod, norm2_g[l:l + 1], final_g, w_out[l], w_mlp1[l], w_mlp2[l],
                            final=(l == DEPTH - 1))
    return x


def kernel(x_prompt, x_sample, c_prompt, c_sample, w_ada, b_ada, norm1_g, w_in, attn_sink,
           ret_decay_fwd, ret_decay_bwd, w_out, norm2_g, w_mlp1, w_mlp2, final_g):
    bp, bs = x_prompt.shape[0], x_sample.shape[0]
    c_all = jnp.concatenate(
        [c_prompt, c_sample, jnp.zeros((MOD_ROWS - bp - bs, D_MODEL), _F32)], axis=0)
    mod_all = _ada_modulation(c_all, w_ada, b_ada)
    decays = jnp.stack([jnp.repeat(ret_decay_fwd, RET_HEAD_DIM, axis=-1),
                        jnp.repeat(ret_decay_bwd, RET_HEAD_DIM, axis=-1)], axis=1)
    w_nat = jnp.concatenate([w_in[:, :, OFF_KR:OFF_VR], w_in[:, :, OFF_KA:OFF_VA]], axis=-1).astype(_BF16)
    w_t = jnp.concatenate(
        [w_in[:, :, OFF_QA:OFF_KA] * (ATTN_HEAD_DIM ** -0.5 * LOG2_E), w_in[:, :, OFF_QR:OFF_KR],
         w_in[:, :, OFF_VR:OFF_GR], w_in[:, :, OFF_GR:IN_WIDTH], w_in[:, :, OFF_VA:OFF_QR]],
        axis=-1).astype(_BF16).transpose(0, 2, 1)
    shared = (norm1_g, w_nat, w_t, attn_sink, decays, w_out.astype(_BF16), norm2_g,
              w_mlp1.astype(_BF16), w_mlp2.astype(_BF16), final_g.reshape(1, D_MODEL))
    y_prompt = _run_trunk(x_prompt, mod_all[:, :bp], *shared)
    y_sample = _run_trunk(x_sample, mod_all[:, bp:bp + bs], *shared)
    return (y_prompt, y_sample)
```

```python
import functools

import jax
import jax.numpy as jnp
from jax import lax
from jax.experimental import pallas as pl
from jax.experimental.pallas import tpu as pltpu

D_MODEL = 1024
DEPTH = 4
ATTN_HEAD_DIM = 64
ATTN_WIDTH = 512
ATTN_HEADS = 8
ATTN_KV_HEADS = 2
ATTN_GROUP = 4
KV_WIDTH = 128
WINDOW = 128
BLOCK = 128
BAND = 3 * BLOCK
RET_WIDTH = 512
RET_HEAD_DIM = 128
RET_HEADS = 4
D_FF = 4 * D_MODEL
N_MOD = 6
EPS = 1e-6
NEG_INF = -1e30
LOG2_E = 1.4426950408889634
OFF_QA, OFF_KA, OFF_VA, OFF_QR, OFF_KR, OFF_VR, OFF_GR, IN_WIDTH = 0, 512, 640, 768, 1280, 1792, 2304, 2816

F_QA, F_QR, F_VR, F_GR, F_VA, F_ROWS = 0, 512, 1024, 1536, 2048, 2176
K_RET, K_ATT, K_COLS = 0, 512, 640

PROJ_TILE = 1024
PROJ_SUB = 512
ROW_TILE = 512
MLP_SUB = 256
FF_CHUNK = 1024
ADA_COLS = 1536
MOD_ROWS = 16
MIX_BLOCKS = 4
STATE_UNROLL = 4
VMEM_LIMIT = 56 * 1024 * 1024

_BF16 = jnp.bfloat16
_F32 = jnp.float32
_BF16_ROWS = 16
_NT = (((1,), (1,)), ((), ()))


def _split_bf16(a):
    hi = a.astype(_BF16)
    lo = (a - hi.astype(_F32)).astype(_BF16)
    return hi, lo


def _params(n_axes):
    return pltpu.CompilerParams(
        dimension_semantics=("arbitrary",) * n_axes, vmem_limit_bytes=VMEM_LIMIT)


def _ada_kernel(c_ref, w_ref, b_ref, o_ref):
    c = c_ref[...]
    act = c * jax.nn.sigmoid(c)
    a_hi, a_lo = _split_bf16(act)
    w_hi, w_lo = _split_bf16(w_ref[0])
    acc = jnp.dot(a_hi, w_hi, preferred_element_type=_F32)
    acc += jnp.dot(a_hi, w_lo, preferred_element_type=_F32)
    acc += jnp.dot(a_lo, w_hi, preferred_element_type=_F32)
    o_ref[0] = acc + b_ref[0]


def _ada_modulation(c_all, w_ada, b_ada):
    n_cols = N_MOD * D_MODEL
    return pl.pallas_call(
        _ada_kernel,
        grid=(DEPTH, n_cols // ADA_COLS),
        in_specs=[
            pl.BlockSpec((MOD_ROWS, D_MODEL), lambda l, j: (0, 0)),
            pl.BlockSpec((1, D_MODEL, ADA_COLS), lambda l, j: (l, 0, j)),
            pl.BlockSpec((1, 1, ADA_COLS), lambda l, j: (l, 0, j)),
        ],
        out_specs=pl.BlockSpec((1, MOD_ROWS, ADA_COLS), lambda l, j: (l, 0, j)),
        out_shape=jax.ShapeDtypeStruct((DEPTH, MOD_ROWS, n_cols), _F32),
        compiler_params=_params(2),
        name="ada_modulation",
    )(c_all, w_ada, b_ada.reshape(DEPTH, 1, n_cols))


_F_GROUPS = ((F_QA, ATTN_WIDTH), (F_QR, RET_WIDTH), (F_VR, RET_WIDTH), (F_GR, RET_WIDTH), (F_VA, KV_WIDTH))


def _modulated_norm(x, gain, shift, scale):
    ms = jnp.mean(x * x, axis=-1, keepdims=True)
    return (x * lax.rsqrt(ms + EPS) * gain) * (1.0 + scale) + shift


def _inproj_kernel(x_ref, mod_ref, g_ref, wn_ref, wt_ref, k_ref, f_ref):
    for t in range(PROJ_TILE // PROJ_SUB):
        rows = slice(t * PROJ_SUB, (t + 1) * PROJ_SUB)
        h = _modulated_norm(x_ref[0, rows, :], g_ref[...], mod_ref[0, 0:1, :], mod_ref[0, 1:2, :]).astype(_BF16)
        k_ref[0, rows, :] = jnp.dot(h, wn_ref[...], preferred_element_type=_F32).astype(_BF16)
        for off, width in _F_GROUPS:
            p = lax.dot_general(wt_ref[off:off + width, :], h, _NT, preferred_element_type=_F32)
            for c in range(PROJ_SUB // BLOCK):
                f_ref[0, t * (PROJ_SUB // BLOCK) + c, off:off + width, :] = (
                    p[:, c * BLOCK:(c + 1) * BLOCK].astype(_BF16))


def _in_projection(x, mod, gain, w_nat, w_t):
    B, S, _ = x.shape
    assert S % PROJ_TILE == 0 and PROJ_TILE % PROJ_SUB == 0 and PROJ_SUB % BLOCK == 0
    blocks = PROJ_TILE // BLOCK
    const = lambda b, i: (0, 0)
    return pl.pallas_call(
        _inproj_kernel,
        grid=(B, S // PROJ_TILE),
        in_specs=[
            pl.BlockSpec((1, PROJ_TILE, D_MODEL), lambda b, i: (b, i, 0)),
            pl.BlockSpec((1, N_MOD, D_MODEL), lambda b, i: (b, 0, 0)),
            pl.BlockSpec((1, D_MODEL), const),
            pl.BlockSpec((D_MODEL, K_COLS), const),
            pl.BlockSpec((F_ROWS, D_MODEL), const),
        ],
        out_specs=[
            pl.BlockSpec((1, PROJ_TILE, K_COLS), lambda b, i: (b, i, 0)),
            pl.BlockSpec((1, blocks, F_ROWS, BLOCK), lambda b, i: (b, i, 0, 0)),
        ],
        out_shape=[
            jax.ShapeDtypeStruct((B, S, K_COLS), _BF16),
            jax.ShapeDtypeStruct((B, S // BLOCK, F_ROWS, BLOCK), _BF16),
        ],
        compiler_params=_params(2),
        name="in_projection",
    )(x, mod, gain, w_nat, w_t)


def _log_decay(decay):
    return jnp.minimum(decay, 0.0) - jnp.log1p(jnp.exp(-jnp.abs(decay)))


def _state_kernel(dec_ref, k_ref, v_ref, rf_ref, rb_ref, s_ref, kvb_ref, ktab_ref, ctab_ref, *, n_chunks):
    s_ref[...] = jnp.zeros_like(s_ref)
    pos = lax.broadcasted_iota(jnp.int32, (BLOCK, 1), 0).astype(_F32)
    for h in range(RET_HEADS):
        cols = slice(h * RET_HEAD_DIM, (h + 1) * RET_HEAD_DIM)
        for d in range(2):
            lg = _log_decay(dec_ref[d:d + 1, cols])
            expo = (BLOCK - 1.0 - pos) if d == 0 else pos
            ktab_ref[h, :, d * RET_HEAD_DIM:(d + 1) * RET_HEAD_DIM] = (RET_HEAD_DIM ** -0.5) * jnp.exp(expo * lg)
            ctab_ref[d, h] = jnp.broadcast_to(jnp.exp(float(BLOCK) * lg), (8, RET_HEAD_DIM))

    def forward(c, carry):
        row0 = pl.multiple_of(c * BLOCK, BLOCK)
        for h in range(RET_HEADS):
            cols = slice(h * RET_HEAD_DIM, (h + 1) * RET_HEAD_DIM)
            k = k_ref[0, pl.ds(row0, BLOCK), cols].astype(_F32)
            k_dec = (jnp.concatenate([k, k], axis=1) * ktab_ref[h]).astype(_BF16)
            kv_t = jnp.dot(v_ref[0, c, cols, :], k_dec, preferred_element_type=_F32)
            state = s_ref[0, h]
            rf_ref[0, c, h] = state.astype(_BF16)
            s_ref[0, h] = state * ctab_ref[0, h, 0:1, :] + kv_t[:, :RET_HEAD_DIM]
            kvb_ref[c, h] = kv_t[:, RET_HEAD_DIM:]
        return carry

    def backward(i, carry):
        c = n_chunks - 1 - i
        for h in range(RET_HEADS):
            state = s_ref[1, h]
            rb_ref[0, c, h] = state.astype(_BF16)
            s_ref[1, h] = state * ctab_ref[1, h, 0:1, :] + kvb_ref[c, h]
        return carry

    lax.fori_loop(0, n_chunks, forward, 0, unroll=STATE_UNROLL)
    lax.fori_loop(0, n_chunks, backward, 0, unroll=STATE_UNROLL)


def _retention_states(k_nat, feat, decays):
    B, S, _ = k_nat.shape
    N = S // BLOCK
    assert N % STATE_UNROLL == 0
    st_shape = jax.ShapeDtypeStruct((B, N, RET_HEADS, RET_HEAD_DIM, RET_HEAD_DIM), _BF16)
    st_spec = pl.BlockSpec((1, N, RET_HEADS, RET_HEAD_DIM, RET_HEAD_DIM), lambda b: (b, 0, 0, 0, 0))
    return pl.pallas_call(
        functools.partial(_state_kernel, n_chunks=N),
        grid=(B,),
        in_specs=[
            pl.BlockSpec((2, RET_WIDTH), lambda b: (0, 0)),
            pl.BlockSpec((1, S, RET_WIDTH), lambda b: (b, 0, K_RET // RET_WIDTH)),
            pl.BlockSpec((1, N, RET_WIDTH, BLOCK), lambda b: (b, 0, F_VR // RET_WIDTH, 0)),
        ],
        out_specs=[st_spec, st_spec],
        out_shape=[st_shape, st_shape],
        scratch_shapes=[
            pltpu.VMEM((2, RET_HEADS, RET_HEAD_DIM, RET_HEAD_DIM), _F32),
            pltpu.VMEM((N, RET_HEADS, RET_HEAD_DIM, RET_HEAD_DIM), _F32),
            pltpu.VMEM((RET_HEADS, BLOCK, 2 * RET_HEAD_DIM), _F32),
            pltpu.VMEM((2, RET_HEADS, 8, RET_HEAD_DIM), _F32),
        ],
        compiler_params=_params(1),
        name="retention_states",
    )(decays, k_nat, feat)


def _mix_block(sub, variant, sink_ref, feat_ref, k_ref, k_blocks, v_blocks, rf_ref, rb_ref, o_ref,
               bias_ref, dcomb_ref, qdec_ref, s_ref, p_ref):
    tok = slice(sub * BLOCK, (sub + 1) * BLOCK)
    kband = jnp.concatenate(k_blocks[sub:sub + 3], axis=0)
    vband = jnp.concatenate(v_blocks[sub:sub + 3], axis=1)

    zeros = jnp.zeros((ATTN_HEAD_DIM, ATTN_GROUP * BLOCK), _BF16)
    q_rows = []
    for kh in range(ATTN_KV_HEADS):
        tiles = [feat_ref[0, sub, F_QA + hh * ATTN_HEAD_DIM:F_QA + (hh + 1) * ATTN_HEAD_DIM, :]
                 for hh in range(kh * ATTN_GROUP, (kh + 1) * ATTN_GROUP)]
        mine = jnp.concatenate(tiles, axis=1)
        q_rows.append(jnp.concatenate([mine, zeros] if kh == 0 else [zeros, mine], axis=1))
    q_bd = jnp.concatenate(q_rows, axis=0)
    s_ref[sub] = jnp.dot(kband, q_bd, preferred_element_type=_F32)

    ret_parts = []
    for h in range(RET_HEADS):
        rows = slice(h * RET_HEAD_DIM, (h + 1) * RET_HEAD_DIM)
        q_t = feat_ref[0, sub, F_QR + h * RET_HEAD_DIM:F_QR + (h + 1) * RET_HEAD_DIM, :]
        s_t = jnp.dot(k_ref[0, tok, rows], q_t, preferred_element_type=_F32)
        cross = jnp.dot(rf_ref[0, sub, h], q_t, preferred_element_type=_F32) * qdec_ref[0, h, 0:1, :]
        cross = cross + jnp.dot(rb_ref[0, sub, h], q_t, preferred_element_type=_F32) * qdec_ref[1, h, 0:1, :]
        ret_parts.append(((s_t * dcomb_ref[h]).astype(_BF16), cross))

    def finish_retention(h):
        a_t, cross = ret_parts[h]
        v_t = feat_ref[0, sub, F_VR + h * RET_HEAD_DIM:F_VR + (h + 1) * RET_HEAD_DIM, :]
        y = jnp.dot(v_t, a_t, preferred_element_type=_F32) + cross
        yc = y - jnp.mean(y, axis=0, keepdims=True)
        yn = yc * lax.rsqrt(jnp.mean(yc * yc, axis=0, keepdims=True) + EPS)
        gate = feat_ref[0, sub, F_GR + h * RET_HEAD_DIM:F_GR + (h + 1) * RET_HEAD_DIM, :].astype(_F32)
        lanes = slice(ATTN_WIDTH + h * RET_HEAD_DIM, ATTN_WIDTH + (h + 1) * RET_HEAD_DIM)
        o_ref[0, tok, lanes] = (gate * jax.nn.sigmoid(gate) * yn).T.astype(_BF16)

    sink_terms = []
    ones = jnp.ones((_BF16_ROWS, BAND), _BF16)

    def softmax_head(hh):
        lanes = slice(hh * BLOCK, (hh + 1) * BLOCK)
        st = s_ref[sub, :, lanes] + bias_ref[variant, hh]
        sink = sink_ref[hh] * LOG2_E
        m = jnp.maximum(jnp.max(st, axis=0, keepdims=True), sink)
        p_ref[sub, :, lanes] = jnp.exp2(st - m).astype(_BF16)
        sink_terms.append(jnp.exp2(sink - m))

    def attend(kh):
        lanes = slice(kh * ATTN_GROUP * BLOCK, (kh + 1) * ATTN_GROUP * BLOCK)
        v_t = jnp.concatenate([vband[kh * ATTN_HEAD_DIM:(kh + 1) * ATTN_HEAD_DIM, :], ones], axis=0)
        o_t = jnp.dot(v_t, p_ref[sub, :, lanes], preferred_element_type=_F32)
        for pair in range(ATTN_GROUP // 2):
            scaled = []
            for g in (2 * pair, 2 * pair + 1):
                q_lanes = slice(g * BLOCK, (g + 1) * BLOCK)
                denom = o_t[ATTN_HEAD_DIM:ATTN_HEAD_DIM + 1, q_lanes] + sink_terms[kh * ATTN_GROUP + g]
                scaled.append(o_t[:ATTN_HEAD_DIM, q_lanes] * (1.0 / denom))
            both = jnp.concatenate(scaled, axis=0)
            first = (kh * ATTN_GROUP + 2 * pair) * ATTN_HEAD_DIM
            o_ref[0, tok, first:first + 2 * ATTN_HEAD_DIM] = both.T.astype(_BF16)

    for hh in range(ATTN_HEADS):
        softmax_head(hh)
        if hh % 2 == 1:
            finish_retention(hh // 2)
        if hh == ATTN_GROUP - 1:
            attend(0)
    attend(1)


def _mix_kernel(sink_ref, dec_ref, feat_ref, k_ref, kp_ref, kn_ref, vp_ref, vn_ref, rf_ref, rb_ref, o_ref,
                bias_ref, dcomb_ref, qdec_ref, s_ref, p_ref, *, n_steps):
    step = pl.program_id(1)

    @pl.when(step == 0)
    def _():
        jj = lax.broadcasted_iota(jnp.int32, (BAND, BLOCK), 0)
        ii = lax.broadcasted_iota(jnp.int32, (BAND, BLOCK), 1)
        dist = jnp.abs(jj - BLOCK - ii)
        distf = dist.astype(_F32)
        has_prev, has_next = jj >= BLOCK, jj < 2 * BLOCK
        for hh in range(ATTN_HEADS):
            slope = 2.0 ** (-8.0 * (hh + 1.0) / ATTN_HEADS)
            table = jnp.where(dist <= WINDOW, (-slope * LOG2_E) * distf, NEG_INF)
            bias_ref[0, hh] = table
            bias_ref[1, hh] = jnp.where(has_prev, table, NEG_INF)
            bias_ref[2, hh] = jnp.where(has_next, table, NEG_INF)
        rj = lax.broadcasted_iota(jnp.int32, (BLOCK, BLOCK), 0)
        ri = lax.broadcasted_iota(jnp.int32, (BLOCK, BLOCK), 1)
        diff = (ri - rj).astype(_F32)
        posq = lax.broadcasted_iota(jnp.int32, (8, BLOCK), 1).astype(_F32)
        for h in range(RET_HEADS):
            cols = slice(h * RET_HEAD_DIM, (h + 1) * RET_HEAD_DIM)
            lg_f = _log_decay(dec_ref[0:1, cols])
            lg_b = _log_decay(dec_ref[1:2, cols])
            d_f = jnp.exp(jnp.maximum(diff, 0.0) * lg_f)
            d_b = jnp.exp(jnp.maximum(-diff, 0.0) * lg_b)
            dcomb_ref[h] = jnp.where(diff >= 0, d_f, d_b) * (RET_HEAD_DIM ** -0.5)
            qdec_ref[0, h] = jnp.exp((posq + 1.0) * lg_f)
            qdec_ref[1, h] = jnp.exp((float(BLOCK) - posq) * lg_b)

    k_blocks = ([kp_ref[0]]
                + [k_ref[0, i * BLOCK:(i + 1) * BLOCK, K_ATT:K_ATT + KV_WIDTH] for i in range(MIX_BLOCKS)]
                + [kn_ref[0]])
    v_blocks = ([vp_ref[0, 0]]
                + [feat_ref[0, i, F_VA:F_VA + KV_WIDTH, :] for i in range(MIX_BLOCKS)]
                + [vn_ref[0, 0]])
    for sub in range(MIX_BLOCKS):
        variant = 0
        if sub == 0:
            variant = jnp.where(step == 0, 1, variant)
        if sub == MIX_BLOCKS - 1:
            variant = jnp.where(step == n_steps - 1, 2, variant)
        _mix_block(sub, variant, sink_ref, feat_ref, k_ref, k_blocks, v_blocks, rf_ref, rb_ref, o_ref,
                   bias_ref, dcomb_ref, qdec_ref, s_ref, p_ref)


def _token_mixing(k_nat, feat, rf, rb, sink, decays):
    B, S, _ = k_nat.shape
    N = S // BLOCK
    n_steps = N // MIX_BLOCKS
    assert n_steps * MIX_BLOCKS == N and N >= 2
    prev_blk = lambda i: jnp.maximum(i * MIX_BLOCKS - 1, 0)
    next_blk = lambda i: jnp.minimum((i + 1) * MIX_BLOCKS, N - 1)
    k_col, v_row = K_ATT // KV_WIDTH, F_VA // KV_WIDTH
    st_spec = pl.BlockSpec((1, MIX_BLOCKS, RET_HEADS, RET_HEAD_DIM, RET_HEAD_DIM), lambda b, i: (b, i, 0, 0, 0))
    return pl.pallas_call(
        functools.partial(_mix_kernel, n_steps=n_steps),
        grid=(B, n_steps),
        in_specs=[
            pl.BlockSpec(memory_space=pltpu.SMEM),
            pl.BlockSpec((2, RET_WIDTH), lambda b, i: (0, 0)),
            pl.BlockSpec((1, MIX_BLOCKS, F_ROWS, BLOCK), lambda b, i: (b, i, 0, 0)),
            pl.BlockSpec((1, MIX_BLOCKS * BLOCK, K_COLS), lambda b, i: (b, i, 0)),
            pl.BlockSpec((1, BLOCK, KV_WIDTH), lambda b, i: (b, prev_blk(i), k_col)),
            pl.BlockSpec((1, BLOCK, KV_WIDTH), lambda b, i: (b, next_blk(i), k_col)),
            pl.BlockSpec((1, 1, KV_WIDTH, BLOCK), lambda b, i: (b, prev_blk(i), v_row, 0)),
            pl.BlockSpec((1, 1, KV_WIDTH, BLOCK), lambda b, i: (b, next_blk(i), v_row, 0)),
            st_spec, st_spec,
        ],
        out_specs=pl.BlockSpec((1, MIX_BLOCKS * BLOCK, D_MODEL), lambda b, i: (b, i, 0)),
        out_shape=jax.ShapeDtypeStruct((B, S, D_MODEL), _BF16),
        scratch_shapes=[
            pltpu.VMEM((3, ATTN_HEADS, BAND, BLOCK), _F32),
            pltpu.VMEM((RET_HEADS, BLOCK, BLOCK), _F32),
            pltpu.VMEM((2, RET_HEADS, 8, BLOCK), _F32),
            pltpu.VMEM((MIX_BLOCKS, BAND, ATTN_HEADS * BLOCK), _F32),
            pltpu.VMEM((MIX_BLOCKS, BAND, ATTN_HEADS * BLOCK), _BF16),
        ],
        compiler_params=_params(2),
        name="token_mixing",
    )(sink, decays, feat, k_nat, k_nat, k_nat, feat, feat, rf, rb)


def _mlp_kernel(x_ref, mix_ref, mod_ref, g2_ref, gf_ref, wo_ref, w1_ref, w2_ref, o_ref, *, final):
    subs = [slice(t * MLP_SUB, (t + 1) * MLP_SUB) for t in range(ROW_TILE // MLP_SUB)]
    projected = [jnp.dot(mix_ref[0, rows, :], wo_ref[...], preferred_element_type=_F32) for rows in subs]
    for rows, proj in zip(subs, projected):
        x1 = x_ref[0, rows, :] + mod_ref[0, 2:3, :] * proj
        h = _modulated_norm(x1, g2_ref[...], mod_ref[0, 3:4, :], mod_ref[0, 4:5, :]).astype(_BF16)
        acc = jnp.zeros((MLP_SUB, D_MODEL), _F32)
        for c in range(D_FF // FF_CHUNK):
            cols = slice(c * FF_CHUNK, (c + 1) * FF_CHUNK)
            a = jnp.dot(h, w1_ref[:, cols], preferred_element_type=_F32)
            a = jnp.square(jnp.maximum(a, 0.0)).astype(_BF16)
            acc = acc + jnp.dot(a, w2_ref[cols, :], preferred_element_type=_F32)
        x2 = x1 + mod_ref[0, 5:6, :] * acc
        if final:
            ms = jnp.mean(x2 * x2, axis=-1, keepdims=True)
            x2 = x2 * lax.rsqrt(ms + EPS) * gf_ref[...]
        o_ref[0, rows, :] = x2


def _channel_mixing(x, mix, mod, gain2, gain_final, w_out, w_mlp1, w_mlp2, final):
    B, S, _ = x.shape
    assert S % ROW_TILE == 0 and ROW_TILE % MLP_SUB == 0
    row = lambda b, i: (b, i, 0)
    const = lambda b, i: (0, 0)
    resident = dict(pipeline_mode=pl.Buffered(1))
    return pl.pallas_call(
        functools.partial(_mlp_kernel, final=final),
        grid=(B, S // ROW_TILE),
        in_specs=[
            pl.BlockSpec((1, ROW_TILE, D_MODEL), row),
            pl.BlockSpec((1, ROW_TILE, D_MODEL), row),
            pl.BlockSpec((1, N_MOD, D_MODEL), lambda b, i: (b, 0, 0)),
            pl.BlockSpec((1, D_MODEL), const),
            pl.BlockSpec((1, D_MODEL), const),
            pl.BlockSpec((D_MODEL, D_MODEL), const, **resident),
            pl.BlockSpec((D_MODEL, D_FF), const, **resident),
            pl.BlockSpec((D_FF, D_MODEL), const, **resident),
        ],
        out_specs=pl.BlockSpec((1, ROW_TILE, D_MODEL), row),
        out_shape=jax.ShapeDtypeStruct((B, S, D_MODEL), _F32),
        compiler_params=_params(2),
        name="channel_mixing",
    )(x, mix, mod, gain2, gain_final, w_out, w_mlp1, w_mlp2)


def _run_trunk(x, mod_rows, norm1_g, w_nat, w_t, attn_sink, decays, w_out, norm2_g, w_mlp1, w_mlp2, final_g):
    B = x.shape[0]
    for l in range(DEPTH):
        mod = mod_rows[l].reshape(B, N_MOD, D_MODEL)
        k_nat, feat = _in_projection(x, mod, norm1_g[l:l + 1], w_nat[l], w_t[l])
        rf, rb = _retention_states(k_nat, feat, decays[l])
        mix = _token_mixing(k_nat, feat, rf, rb, attn_sink[l], decays[l])
        x = _channel_mixing(x, mix, mod, norm2_g[l:l + 1], final_g, w_out[l], w_mlp1[l], w_mlp2[l],
                            final=(l == DEPTH - 1))
    return x


def kernel(x_prompt, x_sample, c_prompt, c_sample, w_ada, b_ada, norm1_g, w_in, attn_sink,
           ret_decay_fwd, ret_decay_bwd, w_out, norm2_g, w_mlp1, w_mlp2, final_g):
    bp, bs = x_prompt.shape[0], x_sample.shape[0]
    c_all = jnp.concatenate(
        [c_prompt, c_sample, jnp.zeros((MOD_ROWS - bp - bs, D_MODEL), _F32)], axis=0)
    mod_all = _ada_modulation(c_all, w_ada, b_ada)
    decays = jnp.stack([jnp.repeat(ret_decay_fwd, RET_HEAD_DIM, axis=-1),
                        jnp.repeat(ret_decay_bwd, RET_HEAD_DIM, axis=-1)], axis=1)
    w_nat = jnp.concatenate([w_in[:, :, OFF_KR:OFF_VR], w_in[:, :, OFF_KA:OFF_VA]], axis=-1).astype(_BF16)
    w_t = jnp.concatenate(
        [w_in[:, :, OFF_QA:OFF_KA] * (ATTN_HEAD_DIM ** -0.5 * LOG2_E), w_in[:, :, OFF_QR:OFF_KR],
         w_in[:, :, OFF_VR:OFF_GR], w_in[:, :, OFF_GR:IN_WIDTH], w_in[:, :, OFF_VA:OFF_QR]],
        axis=-1).astype(_BF16).transpose(0, 2, 1)
    shared = (norm1_g, w_nat, w_t, attn_sink, decays, w_out.astype(_BF16), norm2_g,
              w_mlp1.astype(_BF16), w_mlp2.astype(_BF16), final_g.reshape(1, D_MODEL))
    y_prompt = _run_trunk(x_prompt, mod_all[:, :bp], *shared)
    y_sample = _run_trunk(x_sample, mod_all[:, bp:bp + bs], *shared)
    return (y_prompt, y_sample)
```

```python
import functools

import jax
import jax.numpy as jnp
from jax import lax
from jax.experimental import pallas as pl
from jax.experimental.pallas import tpu as pltpu

D_MODEL = 1024
DEPTH = 4
ATTN_HEAD_DIM = 64
ATTN_WIDTH = 512
ATTN_HEADS = 8
ATTN_KV_HEADS = 2
ATTN_GROUP = 4
KV_WIDTH = 128
WINDOW = 128
BLOCK = 128
BAND = 3 * BLOCK
RET_WIDTH = 512
RET_HEAD_DIM = 128
RET_HEADS = 4
D_FF = 4 * D_MODEL
N_MOD = 6
EPS = 1e-6
NEG_INF = -1e30
LOG2_E = 1.4426950408889634
OFF_QA, OFF_KA, OFF_VA, OFF_QR, OFF_KR, OFF_VR, OFF_GR, IN_WIDTH = 0, 512, 640, 768, 1280, 1792, 2304, 2816

F_QA, F_QR, F_VR, F_GR, F_VA, F_ROWS = 0, 512, 1024, 1536, 2048, 2176
K_RET, K_ATT, K_COLS = 0, 512, 640

PROJ_TILE = 1024
PROJ_SUB = 512
ROW_TILE = 1024
MLP_SUB = 256
FF_CHUNK = 1024
ADA_COLS = 1536
MOD_ROWS = 16
MIX_BLOCKS = 4
STATE_UNROLL = 4
VMEM_LIMIT = 56 * 1024 * 1024

_BF16 = jnp.bfloat16
_F32 = jnp.float32
_BF16_ROWS = 16
_NT = (((1,), (1,)), ((), ()))


def _split_bf16(a):
    hi = a.astype(_BF16)
    lo = (a - hi.astype(_F32)).astype(_BF16)
    return hi, lo


def _params(n_axes):
    return pltpu.CompilerParams(
        dimension_semantics=("arbitrary",) * n_axes, vmem_limit_bytes=VMEM_LIMIT)


def _ada_kernel(c_ref, w_ref, b_ref, o_ref):
    c = c_ref[...]
    act = c * jax.nn.sigmoid(c)
    a_hi, a_lo = _split_bf16(act)
    w_hi, w_lo = _split_bf16(w_ref[0])
    acc = jnp.dot(a_hi, w_hi, preferred_element_type=_F32)
    acc += jnp.dot(a_hi, w_lo, preferred_element_type=_F32)
    acc += jnp.dot(a_lo, w_hi, preferred_element_type=_F32)
    o_ref[0] = acc + b_ref[0]


def _ada_modulation(c_all, w_ada, b_ada):
    n_cols = N_MOD * D_MODEL
    return pl.pallas_call(
        _ada_kernel,
        grid=(DEPTH, n_cols // ADA_COLS),
        in_specs=[
            pl.BlockSpec((MOD_ROWS, D_MODEL), lambda l, j: (0, 0)),
            pl.BlockSpec((1, D_MODEL, ADA_COLS), lambda l, j: (l, 0, j)),
            pl.BlockSpec((1, 1, ADA_COLS), lambda l, j: (l, 0, j)),
        ],
        out_specs=pl.BlockSpec((1, MOD_ROWS, ADA_COLS), lambda l, j: (l, 0, j)),
        out_shape=jax.ShapeDtypeStruct((DEPTH, MOD_ROWS, n_cols), _F32),
        compiler_params=_params(2),
        name="ada_modulation",
    )(c_all, w_ada, b_ada.reshape(DEPTH, 1, n_cols))


_W_GROUPS = (
    (OFF_QA, ATTN_WIDTH, ((F_QA, None),)),
    (OFF_KA, 2 * KV_WIDTH, ((None, K_ATT), (F_VA, None))),
    (OFF_QR, RET_WIDTH, ((F_QR, None),)),
    (OFF_KR, RET_WIDTH, ((None, K_RET),)),
    (OFF_VR, RET_WIDTH, ((F_VR, None),)),
    (OFF_GR, RET_WIDTH, ((F_GR, None),)),
)
Q_SCALE = ATTN_HEAD_DIM ** -0.5 * LOG2_E


def _modulated_norm(x, gain, shift, scale):
    ms = jnp.mean(x * x, axis=-1, keepdims=True)
    return (x * lax.rsqrt(ms + EPS) * gain) * (1.0 + scale) + shift


def _inproj_kernel(x_ref, mod_ref, g_ref, wt_ref, k_ref, f_ref):
    for t in range(PROJ_TILE // PROJ_SUB):
        rows = slice(t * PROJ_SUB, (t + 1) * PROJ_SUB)
        h = _modulated_norm(x_ref[0, rows, :], g_ref[...], mod_ref[0, 0:1, :], mod_ref[0, 1:2, :]).astype(_BF16)
        for w_off, w_rows, dests in _W_GROUPS:
            p = lax.dot_general(wt_ref[w_off:w_off + w_rows, :], h, _NT, preferred_element_type=_F32)
            if w_off == OFF_QA:
                p = p * Q_SCALE
            width = w_rows // len(dests)
            for i, (f_row, k_col) in enumerate(dests):
                part = p[i * width:(i + 1) * width, :]
                if f_row is None:
                    k_ref[0, rows, k_col:k_col + width] = part.T.astype(_BF16)
                else:
                    for c in range(PROJ_SUB // BLOCK):
                        f_ref[0, t * (PROJ_SUB // BLOCK) + c, f_row:f_row + width, :] = (
                            part[:, c * BLOCK:(c + 1) * BLOCK].astype(_BF16))


def _in_projection(x, mod, gain, w_t):
    B, S, _ = x.shape
    assert S % PROJ_TILE == 0 and PROJ_TILE % PROJ_SUB == 0 and PROJ_SUB % BLOCK == 0
    blocks = PROJ_TILE // BLOCK
    const = lambda b, i: (0, 0)
    return pl.pallas_call(
        _inproj_kernel,
        grid=(B, S // PROJ_TILE),
        in_specs=[
            pl.BlockSpec((1, PROJ_TILE, D_MODEL), lambda b, i: (b, i, 0)),
            pl.BlockSpec((1, N_MOD, D_MODEL), lambda b, i: (b, 0, 0)),
            pl.BlockSpec((1, D_MODEL), const),
            pl.BlockSpec((IN_WIDTH, D_MODEL), const),
        ],
        out_specs=[
            pl.BlockSpec((1, PROJ_TILE, K_COLS), lambda b, i: (b, i, 0)),
            pl.BlockSpec((1, blocks, F_ROWS, BLOCK), lambda b, i: (b, i, 0, 0)),
        ],
        out_shape=[
            jax.ShapeDtypeStruct((B, S, K_COLS), _BF16),
            jax.ShapeDtypeStruct((B, S // BLOCK, F_ROWS, BLOCK), _BF16),
        ],
        compiler_params=_params(2),
        name="in_projection",
    )(x, mod, gain, w_t)


def _log_decay(decay):
    return jnp.minimum(decay, 0.0) - jnp.log1p(jnp.exp(-jnp.abs(decay)))


def _state_kernel(dec_ref, k_ref, v_ref, rf_ref, rb_ref, s_ref, kvb_ref, ktab_ref, ctab_ref, *, n_chunks):
    s_ref[...] = jnp.zeros_like(s_ref)
    pos = lax.broadcasted_iota(jnp.int32, (BLOCK, 1), 0).astype(_F32)
    for h in range(RET_HEADS):
        cols = slice(h * RET_HEAD_DIM, (h + 1) * RET_HEAD_DIM)
        for d in range(2):
            lg = _log_decay(dec_ref[d:d + 1, cols])
            expo = (BLOCK - 1.0 - pos) if d == 0 else pos
            ktab_ref[h, :, d * RET_HEAD_DIM:(d + 1) * RET_HEAD_DIM] = (RET_HEAD_DIM ** -0.5) * jnp.exp(expo * lg)
            ctab_ref[d, h] = jnp.broadcast_to(jnp.exp(float(BLOCK) * lg), (8, RET_HEAD_DIM))

    def forward(c, carry):
        row0 = pl.multiple_of(c * BLOCK, BLOCK)
        for h in range(RET_HEADS):
            cols = slice(h * RET_HEAD_DIM, (h + 1) * RET_HEAD_DIM)
            k = k_ref[0, pl.ds(row0, BLOCK), cols].astype(_F32)
            k_dec = (jnp.concatenate([k, k], axis=1) * ktab_ref[h]).astype(_BF16)
            kv_t = jnp.dot(v_ref[0, c, cols, :], k_dec, preferred_element_type=_F32)
            state = s_ref[0, h]
            rf_ref[0, c, h] = state.astype(_BF16)
            s_ref[0, h] = state * ctab_ref[0, h, 0:1, :] + kv_t[:, :RET_HEAD_DIM]
            kvb_ref[c, h] = kv_t[:, RET_HEAD_DIM:]
        return carry

    def backward(i, carry):
        c = n_chunks - 1 - i
        for h in range(RET_HEADS):
            state = s_ref[1, h]
            rb_ref[0, c, h] = state.astype(_BF16)
            s_ref[1, h] = state * ctab_ref[1, h, 0:1, :] + kvb_ref[c, h]
        return carry

    lax.fori_loop(0, n_chunks, forward, 0, unroll=STATE_UNROLL)
    lax.fori_loop(0, n_chunks, backward, 0, unroll=STATE_UNROLL)


def _retention_states(k_nat, feat, decays):
    B, S, _ = k_nat.shape
    N = S // BLOCK
    assert N % STATE_UNROLL == 0
    st_shape = jax.ShapeDtypeStruct((B, N, RET_HEADS, RET_HEAD_DIM, RET_HEAD_DIM), _BF16)
    st_spec = pl.BlockSpec((1, N, RET_HEADS, RET_HEAD_DIM, RET_HEAD_DIM), lambda b: (b, 0, 0, 0, 0))
    return pl.pallas_call(
        functools.partial(_state_kernel, n_chunks=N),
        grid=(B,),
        in_specs=[
            pl.BlockSpec((2, RET_WIDTH), lambda b: (0, 0)),
            pl.BlockSpec((1, S, RET_WIDTH), lambda b: (b, 0, K_RET // RET_WIDTH)),
            pl.BlockSpec((1, N, RET_WIDTH, BLOCK), lambda b: (b, 0, F_VR // RET_WIDTH, 0)),
        ],
        out_specs=[st_spec, st_spec],
        out_shape=[st_shape, st_shape],
        scratch_shapes=[
            pltpu.VMEM((2, RET_HEADS, RET_HEAD_DIM, RET_HEAD_DIM), _F32),
            pltpu.VMEM((N, RET_HEADS, RET_HEAD_DIM, RET_HEAD_DIM), _F32),
            pltpu.VMEM((RET_HEADS, BLOCK, 2 * RET_HEAD_DIM), _F32),
            pltpu.VMEM((2, RET_HEADS, 8, RET_HEAD_DIM), _F32),
        ],
        compiler_params=_params(1),
        name="retention_states",
    )(decays, k_nat, feat)


def _mix_block(sub, variant, sink_ref, feat_ref, k_ref, k_blocks, v_blocks, rf_ref, rb_ref, o_ref,
               bias_ref, dcomb_ref, qdec_ref, s_ref, p_ref):
    tok = slice(sub * BLOCK, (sub + 1) * BLOCK)
    kband = jnp.concatenate(k_blocks[sub:sub + 3], axis=0)
    vband = jnp.concatenate(v_blocks[sub:sub + 3], axis=1)

    zeros = jnp.zeros((ATTN_HEAD_DIM, ATTN_GROUP * BLOCK), _BF16)
    q_rows = []
    for kh in range(ATTN_KV_HEADS):
        tiles = [feat_ref[0, sub, F_QA + hh * ATTN_HEAD_DIM:F_QA + (hh + 1) * ATTN_HEAD_DIM, :]
                 for hh in range(kh * ATTN_GROUP, (kh + 1) * ATTN_GROUP)]
        mine = jnp.concatenate(tiles, axis=1)
        q_rows.append(jnp.concatenate([mine, zeros] if kh == 0 else [zeros, mine], axis=1))
    q_bd = jnp.concatenate(q_rows, axis=0)
    s_ref[sub] = jnp.dot(kband, q_bd, preferred_element_type=_F32)

    ret_parts = {}

    def start_retention(h):
        rows = slice(h * RET_HEAD_DIM, (h + 1) * RET_HEAD_DIM)
        q_t = feat_ref[0, sub, F_QR + h * RET_HEAD_DIM:F_QR + (h + 1) * RET_HEAD_DIM, :]
        s_t = jnp.dot(k_ref[0, tok, rows], q_t, preferred_element_type=_F32)
        cross = jnp.dot(rf_ref[0, sub, h], q_t, preferred_element_type=_F32) * qdec_ref[0, h, 0:1, :]
        cross = cross + jnp.dot(rb_ref[0, sub, h], q_t, preferred_element_type=_F32) * qdec_ref[1, h, 0:1, :]
        ret_parts[h] = ((s_t * dcomb_ref[h]).astype(_BF16), cross)

    def finish_retention(h):
        a_t, cross = ret_parts[h]
        v_t = feat_ref[0, sub, F_VR + h * RET_HEAD_DIM:F_VR + (h + 1) * RET_HEAD_DIM, :]
        y = jnp.dot(v_t, a_t, preferred_element_type=_F32) + cross
        yc = y - jnp.mean(y, axis=0, keepdims=True)
        yn = yc * lax.rsqrt(jnp.mean(yc * yc, axis=0, keepdims=True) + EPS)
        gate = feat_ref[0, sub, F_GR + h * RET_HEAD_DIM:F_GR + (h + 1) * RET_HEAD_DIM, :].astype(_F32)
        lanes = slice(ATTN_WIDTH + h * RET_HEAD_DIM, ATTN_WIDTH + (h + 1) * RET_HEAD_DIM)
        o_ref[0, tok, lanes] = (gate * jax.nn.sigmoid(gate) * yn).T.astype(_BF16)

    sink_terms = []
    ones = jnp.ones((_BF16_ROWS, BAND), _BF16)

    def softmax_head(hh):
        lanes = slice(hh * BLOCK, (hh + 1) * BLOCK)
        st = s_ref[sub, :, lanes] + bias_ref[variant, hh]
        sink = sink_ref[hh] * LOG2_E
        m = jnp.maximum(jnp.max(st, axis=0, keepdims=True), sink)
        p_ref[sub, :, lanes] = jnp.exp2(st - m).astype(_BF16)
        sink_terms.append(jnp.exp2(sink - m))

    def attend(kh):
        lanes = slice(kh * ATTN_GROUP * BLOCK, (kh + 1) * ATTN_GROUP * BLOCK)
        v_t = jnp.concatenate([vband[kh * ATTN_HEAD_DIM:(kh + 1) * ATTN_HEAD_DIM, :], ones], axis=0)
        o_t = jnp.dot(v_t, p_ref[sub, :, lanes], preferred_element_type=_F32)
        for pair in range(ATTN_GROUP // 2):
            scaled = []
            for g in (2 * pair, 2 * pair + 1):
                q_lanes = slice(g * BLOCK, (g + 1) * BLOCK)
                denom = o_t[ATTN_HEAD_DIM:ATTN_HEAD_DIM + 1, q_lanes] + sink_terms[kh * ATTN_GROUP + g]
                scaled.append(o_t[:ATTN_HEAD_DIM, q_lanes] * (1.0 / denom))
            both = jnp.concatenate(scaled, axis=0)
            first = (kh * ATTN_GROUP + 2 * pair) * ATTN_HEAD_DIM
            o_ref[0, tok, first:first + 2 * ATTN_HEAD_DIM] = both.T.astype(_BF16)

    for hh in range(ATTN_HEADS):
        if hh % 2 == 0:
            start_retention(hh // 2)
        softmax_head(hh)
        if hh % 2 == 1:
            finish_retention(hh // 2)
        if hh == ATTN_GROUP - 1:
            attend(0)
    attend(1)


def _mix_kernel(sink_ref, dec_ref, feat_ref, k_ref, kp_ref, kn_ref, vp_ref, vn_ref, rf_ref, rb_ref, o_ref,
                bias_ref, dcomb_ref, qdec_ref, s_ref, p_ref, *, n_steps):
    step = pl.program_id(1)

    @pl.when(step == 0)
    def _():
        jj = lax.broadcasted_iota(jnp.int32, (BAND, BLOCK), 0)
        ii = lax.broadcasted_iota(jnp.int32, (BAND, BLOCK), 1)
        dist = jnp.abs(jj - BLOCK - ii)
        distf = dist.astype(_F32)
        has_prev, has_next = jj >= BLOCK, jj < 2 * BLOCK
        for hh in range(ATTN_HEADS):
            slope = 2.0 ** (-8.0 * (hh + 1.0) / ATTN_HEADS)
            table = jnp.where(dist <= WINDOW, (-slope * LOG2_E) * distf, NEG_INF)
            bias_ref[0, hh] = table
            bias_ref[1, hh] = jnp.where(has_prev, table, NEG_INF)
            bias_ref[2, hh] = jnp.where(has_next, table, NEG_INF)
        rj = lax.broadcasted_iota(jnp.int32, (BLOCK, BLOCK), 0)
        ri = lax.broadcasted_iota(jnp.int32, (BLOCK, BLOCK), 1)
        diff = (ri - rj).astype(_F32)
        posq = lax.broadcasted_iota(jnp.int32, (8, BLOCK), 1).astype(_F32)
        for h in range(RET_HEADS):
            cols = slice(h * RET_HEAD_DIM, (h + 1) * RET_HEAD_DIM)
            lg_f = _log_decay(dec_ref[0:1, cols])
            lg_b = _log_decay(dec_ref[1:2, cols])
            d_f = jnp.exp(jnp.maximum(diff, 0.0) * lg_f)
            d_b = jnp.exp(jnp.maximum(-diff, 0.0) * lg_b)
            dcomb_ref[h] = jnp.where(diff >= 0, d_f, d_b) * (RET_HEAD_DIM ** -0.5)
            qdec_ref[0, h] = jnp.exp((posq + 1.0) * lg_f)
            qdec_ref[1, h] = jnp.exp((float(BLOCK) - posq) * lg_b)

    k_blocks = ([kp_ref[0]]
                + [k_ref[0, i * BLOCK:(i + 1) * BLOCK, K_ATT:K_ATT + KV_WIDTH] for i in range(MIX_BLOCKS)]
                + [kn_ref[0]])
    v_blocks = ([vp_ref[0, 0]]
                + [feat_ref[0, i, F_VA:F_VA + KV_WIDTH, :] for i in range(MIX_BLOCKS)]
                + [vn_ref[0, 0]])
    for sub in range(MIX_BLOCKS):
        variant = 0
        if sub == 0:
            variant = jnp.where(step == 0, 1, variant)
        if sub == MIX_BLOCKS - 1:
            variant = jnp.where(step == n_steps - 1, 2, variant)
        _mix_block(sub, variant, sink_ref, feat_ref, k_ref, k_blocks, v_blocks, rf_ref, rb_ref, o_ref,
                   bias_ref, dcomb_ref, qdec_ref, s_ref, p_ref)


def _token_mixing(k_nat, feat, rf, rb, sink, decays):
    B, S, _ = k_nat.shape
    N = S // BLOCK
    n_steps = N // MIX_BLOCKS
    assert n_steps * MIX_BLOCKS == N and N >= 2
    prev_blk = lambda i: jnp.maximum(i * MIX_BLOCKS - 1, 0)
    next_blk = lambda i: jnp.minimum((i + 1) * MIX_BLOCKS, N - 1)
    k_col, v_row = K_ATT // KV_WIDTH, F_VA // KV_WIDTH
    st_spec = pl.BlockSpec((1, MIX_BLOCKS, RET_HEADS, RET_HEAD_DIM, RET_HEAD_DIM), lambda b, i: (b, i, 0, 0, 0))
    return pl.pallas_call(
        functools.partial(_mix_kernel, n_steps=n_steps),
        grid=(B, n_steps),
        in_specs=[
            pl.BlockSpec(memory_space=pltpu.SMEM),
            pl.BlockSpec((2, RET_WIDTH), lambda b, i: (0, 0)),
            pl.BlockSpec((1, MIX_BLOCKS, F_ROWS, BLOCK), lambda b, i: (b, i, 0, 0)),
            pl.BlockSpec((1, MIX_BLOCKS * BLOCK, K_COLS), lambda b, i: (b, i, 0)),
            pl.BlockSpec((1, BLOCK, KV_WIDTH), lambda b, i: (b, prev_blk(i), k_col)),
            pl.BlockSpec((1, BLOCK, KV_WIDTH), lambda b, i: (b, next_blk(i), k_col)),
            pl.BlockSpec((1, 1, KV_WIDTH, BLOCK), lambda b, i: (b, prev_blk(i), v_row, 0)),
            pl.BlockSpec((1, 1, KV_WIDTH, BLOCK), lambda b, i: (b, next_blk(i), v_row, 0)),
            st_spec, st_spec,
        ],
        out_specs=pl.BlockSpec((1, MIX_BLOCKS * BLOCK, D_MODEL), lambda b, i: (b, i, 0)),
        out_shape=jax.ShapeDtypeStruct((B, S, D_MODEL), _BF16),
        scratch_shapes=[
            pltpu.VMEM((3, ATTN_HEADS, BAND, BLOCK), _F32),
            pltpu.VMEM((RET_HEADS, BLOCK, BLOCK), _F32),
            pltpu.VMEM((2, RET_HEADS, 8, BLOCK), _F32),
            pltpu.VMEM((MIX_BLOCKS, BAND, ATTN_HEADS * BLOCK), _F32),
            pltpu.VMEM((MIX_BLOCKS, BAND, ATTN_HEADS * BLOCK), _BF16),
        ],
        compiler_params=_params(2),
        name="token_mixing",
    )(sink, decays, feat, k_nat, k_nat, k_nat, feat, feat, rf, rb)


def _mlp_kernel(x_ref, mix_ref, mod_ref, g2_ref, gf_ref, wo_ref, w1_ref, w2_ref, o_ref, *, final):
    subs = [slice(t * MLP_SUB, (t + 1) * MLP_SUB) for t in range(ROW_TILE // MLP_SUB)]
    projected = [jnp.dot(mix_ref[0, rows, :], wo_ref[...], preferred_element_type=_F32) for rows in subs]
    for rows, proj in zip(subs, projected):
        x1 = x_ref[0, rows, :] + mod_ref[0, 2:3, :] * proj
        h = _modulated_norm(x1, g2_ref[...], mod_ref[0, 3:4, :], mod_ref[0, 4:5, :]).astype(_BF16)
        acc = jnp.zeros((MLP_SUB, D_MODEL), _F32)
        for c in range(D_FF // FF_CHUNK):
            cols = slice(c * FF_CHUNK, (c + 1) * FF_CHUNK)
            a = jnp.dot(h, w1_ref[:, cols], preferred_element_type=_F32)
            a = jnp.square(jnp.maximum(a, 0.0)).astype(_BF16)
            acc = acc + jnp.dot(a, w2_ref[cols, :], preferred_element_type=_F32)
        x2 = x1 + mod_ref[0, 5:6, :] * acc
        if final:
            ms = jnp.mean(x2 * x2, axis=-1, keepdims=True)
            x2 = x2 * lax.rsqrt(ms + EPS) * gf_ref[...]
        o_ref[0, rows, :] = x2


def _channel_mixing(x, mix, mod, gain2, gain_final, w_out, w_mlp1, w_mlp2, final):
    B, S, _ = x.shape
    assert S % ROW_TILE == 0 and ROW_TILE % MLP_SUB == 0
    row = lambda b, i: (b, i, 0)
    const = lambda b, i: (0, 0)
    resident = dict(pipeline_mode=pl.Buffered(1))
    return pl.pallas_call(
        functools.partial(_mlp_kernel, final=final),
        grid=(B, S // ROW_TILE),
        in_specs=[
            pl.BlockSpec((1, ROW_TILE, D_MODEL), row),
            pl.BlockSpec((1, ROW_TILE, D_MODEL), row),
            pl.BlockSpec((1, N_MOD, D_MODEL), lambda b, i: (b, 0, 0)),
            pl.BlockSpec((1, D_MODEL), const),
            pl.BlockSpec((1, D_MODEL), const),
            pl.BlockSpec((D_MODEL, D_MODEL), const, **resident),
            pl.BlockSpec((D_MODEL, D_FF), const, **resident),
            pl.BlockSpec((D_FF, D_MODEL), const, **resident),
        ],
        out_specs=pl.BlockSpec((1, ROW_TILE, D_MODEL), row),
        out_shape=jax.ShapeDtypeStruct((B, S, D_MODEL), _F32),
        compiler_params=_params(2),
        name="channel_mixing",
    )(x, mix, mod, gain2, gain_final, w_out, w_mlp1, w_mlp2)


def _run_trunk(x, mod_rows, norm1_g, w_t, attn_sink, decays, w_out, norm2_g, w_mlp1, w_mlp2, final_g):
    B = x.shape[0]
    for l in range(DEPTH):
        mod = mod_rows[l].reshape(B, N_MOD, D_MODEL)
        k_nat, feat = _in_projection(x, mod, norm1_g[l:l + 1], w_t[l])
        rf, rb = _retention_states(k_nat, feat, decays[l])
        mix = _token_mixing(k_nat, feat, rf, rb, attn_sink[l], decays[l])
        x = _channel_mixing(x, mix, mod, norm2_g[l:l + 1], final_g, w_out[l], w_mlp1[l], w_mlp2[l],
                            final=(l == DEPTH - 1))
    return x


def kernel(x_prompt, x_sample, c_prompt, c_sample, w_ada, b_ada, norm1_g, w_in, attn_sink,
           ret_decay_fwd, ret_decay_bwd, w_out, norm2_g, w_mlp1, w_mlp2, final_g):
    bp, bs = x_prompt.shape[0], x_sample.shape[0]
    c_all = jnp.concatenate(
        [c_prompt, c_sample, jnp.zeros((MOD_ROWS - bp - bs, D_MODEL), _F32)], axis=0)
    mod_all = _ada_modulation(c_all, w_ada, b_ada)
    decays = jnp.stack([jnp.repeat(ret_decay_fwd, RET_HEAD_DIM, axis=-1),
                        jnp.repeat(ret_decay_bwd, RET_HEAD_DIM, axis=-1)], axis=1)
    w_t = w_in.astype(_BF16).transpose(0, 2, 1)
    shared = (norm1_g, w_t, attn_sink, decays, w_out.astype(_BF16), norm2_g,
              w_mlp1.astype(_BF16), w_mlp2.astype(_BF16), final_g.reshape(1, D_MODEL))
    y_prompt = _run_trunk(x_prompt, mod_all[:, :bp], *shared)
    y_sample = _run_trunk(x_sample, mod_all[:, bp:bp + bs], *shared)
    return (y_prompt, y_sample)
```

```python
import functools

import jax
import jax.numpy as jnp
from jax import lax
from jax.experimental import pallas as pl
from jax.experimental.pallas import tpu as pltpu

D_MODEL = 1024
DEPTH = 4
ATTN_HEAD_DIM = 64
ATTN_WIDTH = 512
ATTN_HEADS = 8
ATTN_KV_HEADS = 2
ATTN_GROUP = 4
KV_WIDTH = 128
WINDOW = 128
BLOCK = 128
BAND = 3 * BLOCK
RET_WIDTH = 512
RET_HEAD_DIM = 128
RET_HEADS = 4
D_FF = 4 * D_MODEL
N_MOD = 6
EPS = 1e-6
NEG_INF = -1e30
LOG2_E = 1.4426950408889634
OFF_QA, OFF_KA, OFF_VA, OFF_QR, OFF_KR, OFF_VR, OFF_GR, IN_WIDTH = 0, 512, 640, 768, 1280, 1792, 2304, 2816

F_QA, F_QR, F_VR, F_GR, F_VA, F_ROWS = 0, 512, 1024, 1536, 2048, 2176
K_RET, K_ATT, K_COLS = 0, 512, 640

PROJ_TILE = 1024
PROJ_SUB = 512
ROW_TILE = 1024
MLP_SUB = 256
FF_CHUNK = 1024
ADA_COLS = 1536
MOD_ROWS = 16
MIX_BLOCKS = 4
STATE_UNROLL = 4
VMEM_LIMIT = 56 * 1024 * 1024

_BF16 = jnp.bfloat16
_F32 = jnp.float32
_BF16_ROWS = 16
_NT = (((1,), (1,)), ((), ()))


def _split_bf16(a):
    hi = a.astype(_BF16)
    lo = (a - hi.astype(_F32)).astype(_BF16)
    return hi, lo


def _params(n_axes):
    return pltpu.CompilerParams(
        dimension_semantics=("arbitrary",) * n_axes, vmem_limit_bytes=VMEM_LIMIT)


def _ada_kernel(c_ref, w_ref, b_ref, o_ref):
    c = c_ref[...]
    act = c * jax.nn.sigmoid(c)
    a_hi, a_lo = _split_bf16(act)
    w_hi, w_lo = _split_bf16(w_ref[0])
    acc = jnp.dot(a_hi, w_hi, preferred_element_type=_F32)
    acc += jnp.dot(a_hi, w_lo, preferred_element_type=_F32)
    acc += jnp.dot(a_lo, w_hi, preferred_element_type=_F32)
    o_ref[0] = acc + b_ref[0]


def _ada_modulation(c_all, w_ada, b_ada):
    n_cols = N_MOD * D_MODEL
    return pl.pallas_call(
        _ada_kernel,
        grid=(DEPTH, n_cols // ADA_COLS),
        in_specs=[
            pl.BlockSpec((MOD_ROWS, D_MODEL), lambda l, j: (0, 0)),
            pl.BlockSpec((1, D_MODEL, ADA_COLS), lambda l, j: (l, 0, j)),
            pl.BlockSpec((1, 1, ADA_COLS), lambda l, j: (l, 0, j)),
        ],
        out_specs=pl.BlockSpec((1, MOD_ROWS, ADA_COLS), lambda l, j: (l, 0, j)),
        out_shape=jax.ShapeDtypeStruct((DEPTH, MOD_ROWS, n_cols), _F32),
        compiler_params=_params(2),
        name="ada_modulation",
    )(c_all, w_ada, b_ada.reshape(DEPTH, 1, n_cols))


_W_GROUPS = (
    (OFF_QA, ATTN_WIDTH, ((F_QA, None),)),
    (OFF_KA, 2 * KV_WIDTH, ((None, K_ATT), (F_VA, None))),
    (OFF_QR, RET_WIDTH, ((F_QR, None),)),
    (OFF_KR, RET_WIDTH, ((None, K_RET),)),
    (OFF_VR, RET_WIDTH, ((F_VR, None),)),
    (OFF_GR, RET_WIDTH, ((F_GR, None),)),
)
Q_SCALE = ATTN_HEAD_DIM ** -0.5 * LOG2_E


def _modulated_norm(x, gain, shift, scale):
    ms = jnp.mean(x * x, axis=-1, keepdims=True)
    return (x * lax.rsqrt(ms + EPS) * gain) * (1.0 + scale) + shift


def _inproj_kernel(x_ref, mod_ref, g_ref, wt_ref, k_ref, f_ref):
    for t in range(PROJ_TILE // PROJ_SUB):
        rows = slice(t * PROJ_SUB, (t + 1) * PROJ_SUB)
        h = _modulated_norm(x_ref[0, rows, :], g_ref[0], mod_ref[0, 0:1, :], mod_ref[0, 1:2, :]).astype(_BF16)
        for w_off, w_rows, dests in _W_GROUPS:
            p = lax.dot_general(wt_ref[0, w_off:w_off + w_rows, :], h, _NT, preferred_element_type=_F32)
            if w_off == OFF_QA:
                p = p * Q_SCALE
            width = w_rows // len(dests)
            for i, (f_row, k_col) in enumerate(dests):
                part = p[i * width:(i + 1) * width, :]
                if f_row is None:
                    k_ref[0, rows, k_col:k_col + width] = part.T.astype(_BF16)
                else:
                    for c in range(PROJ_SUB // BLOCK):
                        f_ref[0, t * (PROJ_SUB // BLOCK) + c, f_row:f_row + width, :] = (
                            part[:, c * BLOCK:(c + 1) * BLOCK].astype(_BF16))


def _in_projection(x, mod, gains, w_t, layer):
    B, S, _ = x.shape
    assert S % PROJ_TILE == 0 and PROJ_TILE % PROJ_SUB == 0 and PROJ_SUB % BLOCK == 0
    blocks = PROJ_TILE // BLOCK
    this_layer = lambda b, i: (layer, 0, 0)
    return pl.pallas_call(
        _inproj_kernel,
        grid=(B, S // PROJ_TILE),
        in_specs=[
            pl.BlockSpec((1, PROJ_TILE, D_MODEL), lambda b, i: (b, i, 0)),
            pl.BlockSpec((1, N_MOD, D_MODEL), lambda b, i: (b, 0, 0)),
            pl.BlockSpec((1, 1, D_MODEL), this_layer),
            pl.BlockSpec((1, IN_WIDTH, D_MODEL), this_layer),
        ],
        out_specs=[
            pl.BlockSpec((1, PROJ_TILE, K_COLS), lambda b, i: (b, i, 0)),
            pl.BlockSpec((1, blocks, F_ROWS, BLOCK), lambda b, i: (b, i, 0, 0)),
        ],
        out_shape=[
            jax.ShapeDtypeStruct((B, S, K_COLS), _BF16),
            jax.ShapeDtypeStruct((B, S // BLOCK, F_ROWS, BLOCK), _BF16),
        ],
        compiler_params=_params(2),
        name="in_projection",
    )(x, mod, gains, w_t)


def _log_decay(decay):
    return jnp.minimum(decay, 0.0) - jnp.log1p(jnp.exp(-jnp.abs(decay)))


def _state_kernel(dec_ref, k_ref, v_ref, rf_ref, rb_ref, s_ref, kvb_ref, ktab_ref, ctab_ref, *, n_chunks):
    s_ref[...] = jnp.zeros_like(s_ref)
    pos = lax.broadcasted_iota(jnp.int32, (BLOCK, 1), 0).astype(_F32)
    for h in range(RET_HEADS):
        cols = slice(h * RET_HEAD_DIM, (h + 1) * RET_HEAD_DIM)
        for d in range(2):
            lg = _log_decay(dec_ref[d:d + 1, cols])
            expo = (BLOCK - 1.0 - pos) if d == 0 else pos
            ktab_ref[h, :, d * RET_HEAD_DIM:(d + 1) * RET_HEAD_DIM] = (RET_HEAD_DIM ** -0.5) * jnp.exp(expo * lg)
            ctab_ref[d, h] = jnp.broadcast_to(jnp.exp(float(BLOCK) * lg), (8, RET_HEAD_DIM))

    def forward(c, carry):
        row0 = pl.multiple_of(c * BLOCK, BLOCK)
        for h in range(RET_HEADS):
            cols = slice(h * RET_HEAD_DIM, (h + 1) * RET_HEAD_DIM)
            k = k_ref[0, pl.ds(row0, BLOCK), cols].astype(_F32)
            k_dec = (jnp.concatenate([k, k], axis=1) * ktab_ref[h]).astype(_BF16)
            kv_t = jnp.dot(v_ref[0, c, cols, :], k_dec, preferred_element_type=_F32)
            state = s_ref[0, h]
            rf_ref[0, c, h] = state.astype(_BF16)
            s_ref[0, h] = state * ctab_ref[0, h, 0:1, :] + kv_t[:, :RET_HEAD_DIM]
            kvb_ref[c, h] = kv_t[:, RET_HEAD_DIM:]
        return carry

    def backward(i, carry):
        c = n_chunks - 1 - i
        for h in range(RET_HEADS):
            state = s_ref[1, h]
            rb_ref[0, c, h] = state.astype(_BF16)
            s_ref[1, h] = state * ctab_ref[1, h, 0:1, :] + kvb_ref[c, h]
        return carry

    lax.fori_loop(0, n_chunks, forward, 0, unroll=STATE_UNROLL)
    lax.fori_loop(0, n_chunks, backward, 0, unroll=STATE_UNROLL)


def _retention_states(k_nat, feat, decays):
    B, S, _ = k_nat.shape
    N = S // BLOCK
    assert N % STATE_UNROLL == 0
    st_shape = jax.ShapeDtypeStruct((B, N, RET_HEADS, RET_HEAD_DIM, RET_HEAD_DIM), _BF16)
    st_spec = pl.BlockSpec((1, N, RET_HEADS, RET_HEAD_DIM, RET_HEAD_DIM), lambda b: (b, 0, 0, 0, 0))
    return pl.pallas_call(
        functools.partial(_state_kernel, n_chunks=N),
        grid=(B,),
        in_specs=[
            pl.BlockSpec((2, RET_WIDTH), lambda b: (0, 0)),
            pl.BlockSpec((1, S, RET_WIDTH), lambda b: (b, 0, K_RET // RET_WIDTH)),
            pl.BlockSpec((1, N, RET_WIDTH, BLOCK), lambda b: (b, 0, F_VR // RET_WIDTH, 0)),
        ],
        out_specs=[st_spec, st_spec],
        out_shape=[st_shape, st_shape],
        scratch_shapes=[
            pltpu.VMEM((2, RET_HEADS, RET_HEAD_DIM, RET_HEAD_DIM), _F32),
            pltpu.VMEM((N, RET_HEADS, RET_HEAD_DIM, RET_HEAD_DIM), _F32),
            pltpu.VMEM((RET_HEADS, BLOCK, 2 * RET_HEAD_DIM), _F32),
            pltpu.VMEM((2, RET_HEADS, 8, RET_HEAD_DIM), _F32),
        ],
        compiler_params=_params(1),
        name="retention_states",
    )(decays, k_nat, feat)


def _mix_block(sub, variant, sink_ref, feat_ref, k_ref, k_blocks, v_blocks, rf_ref, rb_ref, o_ref,
               bias_ref, dcomb_ref, qdec_ref, s_ref, p_ref):
    tok = slice(sub * BLOCK, (sub + 1) * BLOCK)
    kband = jnp.concatenate(k_blocks[sub:sub + 3], axis=0)
    vband = jnp.concatenate(v_blocks[sub:sub + 3], axis=1)

    zeros = jnp.zeros((ATTN_HEAD_DIM, ATTN_GROUP * BLOCK), _BF16)
    q_rows = []
    for kh in range(ATTN_KV_HEADS):
        tiles = [feat_ref[0, sub, F_QA + hh * ATTN_HEAD_DIM:F_QA + (hh + 1) * ATTN_HEAD_DIM, :]
                 for hh in range(kh * ATTN_GROUP, (kh + 1) * ATTN_GROUP)]
        mine = jnp.concatenate(tiles, axis=1)
        q_rows.append(jnp.concatenate([mine, zeros] if kh == 0 else [zeros, mine], axis=1))
    q_bd = jnp.concatenate(q_rows, axis=0)
    scores = jnp.dot(kband, q_bd, preferred_element_type=_F32)
    for hh in range(ATTN_HEADS):
        s_ref[sub, hh] = scores[:, hh * BLOCK:(hh + 1) * BLOCK]

    ret_parts = {}

    def start_retention(h):
        rows = slice(h * RET_HEAD_DIM, (h + 1) * RET_HEAD_DIM)
        q_t = feat_ref[0, sub, F_QR + h * RET_HEAD_DIM:F_QR + (h + 1) * RET_HEAD_DIM, :]
        s_t = jnp.dot(k_ref[0, tok, rows], q_t, preferred_element_type=_F32)
        cross = jnp.dot(rf_ref[0, sub, h], q_t, preferred_element_type=_F32) * qdec_ref[0, h, 0:1, :]
        cross = cross + jnp.dot(rb_ref[0, sub, h], q_t, preferred_element_type=_F32) * qdec_ref[1, h, 0:1, :]
        ret_parts[h] = ((s_t * dcomb_ref[h]).astype(_BF16), cross)

    def finish_retention(h):
        a_t, cross = ret_parts[h]
        v_t = feat_ref[0, sub, F_VR + h * RET_HEAD_DIM:F_VR + (h + 1) * RET_HEAD_DIM, :]
        y = jnp.dot(v_t, a_t, preferred_element_type=_F32) + cross
        yc = y - jnp.mean(y, axis=0, keepdims=True)
        yn = yc * lax.rsqrt(jnp.mean(yc * yc, axis=0, keepdims=True) + EPS)
        gate = feat_ref[0, sub, F_GR + h * RET_HEAD_DIM:F_GR + (h + 1) * RET_HEAD_DIM, :].astype(_F32)
        lanes = slice(ATTN_WIDTH + h * RET_HEAD_DIM, ATTN_WIDTH + (h + 1) * RET_HEAD_DIM)
        o_ref[0, tok, lanes] = (gate * jax.nn.sigmoid(gate) * yn).T.astype(_BF16)

    sink_terms = []
    ones = jnp.ones((_BF16_ROWS, BAND), _BF16)

    def softmax_head(hh):
        st = s_ref[sub, hh] + bias_ref[variant, hh]
        sink = sink_ref[hh] * LOG2_E
        m = jnp.maximum(jnp.max(st, axis=0, keepdims=True), sink)
        p_ref[sub, hh] = jnp.exp2(st - m).astype(_BF16)
        sink_terms.append(jnp.exp2(sink - m))

    def attend(kh):
        v_t = jnp.concatenate([vband[kh * ATTN_HEAD_DIM:(kh + 1) * ATTN_HEAD_DIM, :], ones], axis=0)
        probs = jnp.concatenate(
            [p_ref[sub, hh] for hh in range(kh * ATTN_GROUP, (kh + 1) * ATTN_GROUP)], axis=1)
        o_t = jnp.dot(v_t, probs, preferred_element_type=_F32)
        for pair in range(ATTN_GROUP // 2):
            scaled = []
            for g in (2 * pair, 2 * pair + 1):
                q_lanes = slice(g * BLOCK, (g + 1) * BLOCK)
                denom = o_t[ATTN_HEAD_DIM:ATTN_HEAD_DIM + 1, q_lanes] + sink_terms[kh * ATTN_GROUP + g]
                scaled.append(o_t[:ATTN_HEAD_DIM, q_lanes] * (1.0 / denom))
            both = jnp.concatenate(scaled, axis=0)
            first = (kh * ATTN_GROUP + 2 * pair) * ATTN_HEAD_DIM
            o_ref[0, tok, first:first + 2 * ATTN_HEAD_DIM] = both.T.astype(_BF16)

    for hh in range(ATTN_HEADS):
        if hh % 2 == 0:
            start_retention(hh // 2)
        softmax_head(hh)
        if hh % 2 == 1:
            finish_retention(hh // 2)
        if hh == ATTN_GROUP - 1:
            attend(0)
    attend(1)


def _mix_kernel(sink_ref, dec_ref, feat_ref, k_ref, kp_ref, kn_ref, vp_ref, vn_ref, rf_ref, rb_ref, o_ref,
                bias_ref, dcomb_ref, qdec_ref, s_ref, p_ref, *, n_steps):
    step = pl.program_id(1)

    @pl.when(step == 0)
    def _():
        jj = lax.broadcasted_iota(jnp.int32, (BAND, BLOCK), 0)
        ii = lax.broadcasted_iota(jnp.int32, (BAND, BLOCK), 1)
        dist = jnp.abs(jj - BLOCK - ii)
        distf = dist.astype(_F32)
        has_prev, has_next = jj >= BLOCK, jj < 2 * BLOCK
        for hh in range(ATTN_HEADS):
            slope = 2.0 ** (-8.0 * (hh + 1.0) / ATTN_HEADS)
            table = jnp.where(dist <= WINDOW, (-slope * LOG2_E) * distf, NEG_INF)
            bias_ref[0, hh] = table
            bias_ref[1, hh] = jnp.where(has_prev, table, NEG_INF)
            bias_ref[2, hh] = jnp.where(has_next, table, NEG_INF)
        rj = lax.broadcasted_iota(jnp.int32, (BLOCK, BLOCK), 0)
        ri = lax.broadcasted_iota(jnp.int32, (BLOCK, BLOCK), 1)
        diff = (ri - rj).astype(_F32)
        posq = lax.broadcasted_iota(jnp.int32, (8, BLOCK), 1).astype(_F32)
        for h in range(RET_HEADS):
            cols = slice(h * RET_HEAD_DIM, (h + 1) * RET_HEAD_DIM)
            lg_f = _log_decay(dec_ref[0:1, cols])
            lg_b = _log_decay(dec_ref[1:2, cols])
            d_f = jnp.exp(jnp.maximum(diff, 0.0) * lg_f)
            d_b = jnp.exp(jnp.maximum(-diff, 0.0) * lg_b)
            dcomb_ref[h] = jnp.where(diff >= 0, d_f, d_b) * (RET_HEAD_DIM ** -0.5)
            qdec_ref[0, h] = jnp.exp((posq + 1.0) * lg_f)
            qdec_ref[1, h] = jnp.exp((float(BLOCK) - posq) * lg_b)

    k_blocks = ([kp_ref[0]]
                + [k_ref[0, i * BLOCK:(i + 1) * BLOCK, K_ATT:K_ATT + KV_WIDTH] for i in range(MIX_BLOCKS)]
                + [kn_ref[0]])
    v_blocks = ([vp_ref[0, 0]]
                + [feat_ref[0, i, F_VA:F_VA + KV_WIDTH, :] for i in range(MIX_BLOCKS)]
                + [vn_ref[0, 0]])
    for sub in range(MIX_BLOCKS):
        variant = 0
        if sub == 0:
            variant = jnp.where(step == 0, 1, variant)
        if sub == MIX_BLOCKS - 1:
            variant = jnp.where(step == n_steps - 1, 2, variant)
        _mix_block(sub, variant, sink_ref, feat_ref, k_ref, k_blocks, v_blocks, rf_ref, rb_ref, o_ref,
                   bias_ref, dcomb_ref, qdec_ref, s_ref, p_ref)


def _token_mixing(k_nat, feat, rf, rb, sink, decays):
    B, S, _ = k_nat.shape
    N = S // BLOCK
    n_steps = N // MIX_BLOCKS
    assert n_steps * MIX_BLOCKS == N and N >= 2
    prev_blk = lambda i: jnp.maximum(i * MIX_BLOCKS - 1, 0)
    next_blk = lambda i: jnp.minimum((i + 1) * MIX_BLOCKS, N - 1)
    k_col, v_row = K_ATT // KV_WIDTH, F_VA // KV_WIDTH
    st_spec = pl.BlockSpec((1, MIX_BLOCKS, RET_HEADS, RET_HEAD_DIM, RET_HEAD_DIM), lambda b, i: (b, i, 0, 0, 0))
    return pl.pallas_call(
        functools.partial(_mix_kernel, n_steps=n_steps),
        grid=(B, n_steps),
        in_specs=[
            pl.BlockSpec(memory_space=pltpu.SMEM),
            pl.BlockSpec((2, RET_WIDTH), lambda b, i: (0, 0)),
            pl.BlockSpec((1, MIX_BLOCKS, F_ROWS, BLOCK), lambda b, i: (b, i, 0, 0)),
            pl.BlockSpec((1, MIX_BLOCKS * BLOCK, K_COLS), lambda b, i: (b, i, 0)),
            pl.BlockSpec((1, BLOCK, KV_WIDTH), lambda b, i: (b, prev_blk(i), k_col)),
            pl.BlockSpec((1, BLOCK, KV_WIDTH), lambda b, i: (b, next_blk(i), k_col)),
            pl.BlockSpec((1, 1, KV_WIDTH, BLOCK), lambda b, i: (b, prev_blk(i), v_row, 0)),
            pl.BlockSpec((1, 1, KV_WIDTH, BLOCK), lambda b, i: (b, next_blk(i), v_row, 0)),
            st_spec, st_spec,
        ],
        out_specs=pl.BlockSpec((1, MIX_BLOCKS * BLOCK, D_MODEL), lambda b, i: (b, i, 0)),
        out_shape=jax.ShapeDtypeStruct((B, S, D_MODEL), _BF16),
        scratch_shapes=[
            pltpu.VMEM((3, ATTN_HEADS, BAND, BLOCK), _F32),
            pltpu.VMEM((RET_HEADS, BLOCK, BLOCK), _F32),
            pltpu.VMEM((2, RET_HEADS, 8, BLOCK), _F32),
            pltpu.VMEM((MIX_BLOCKS, ATTN_HEADS, BAND, BLOCK), _F32),
            pltpu.VMEM((MIX_BLOCKS, ATTN_HEADS, BAND, BLOCK), _BF16),
        ],
        compiler_params=_params(2),
        name="token_mixing",
    )(sink, decays, feat, k_nat, k_nat, k_nat, feat, feat, rf, rb)


def _mlp_kernel(x_ref, mix_ref, mod_ref, g2_ref, gf_ref, wo_ref, w1_ref, w2_ref, o_ref, *, final):
    subs = [slice(t * MLP_SUB, (t + 1) * MLP_SUB) for t in range(ROW_TILE // MLP_SUB)]
    projected = [jnp.dot(mix_ref[0, rows, :], wo_ref[0], preferred_element_type=_F32) for rows in subs]
    for rows, proj in zip(subs, projected):
        x1 = x_ref[0, rows, :] + mod_ref[0, 2:3, :] * proj
        h = _modulated_norm(x1, g2_ref[0], mod_ref[0, 3:4, :], mod_ref[0, 4:5, :]).astype(_BF16)
        acc = jnp.zeros((MLP_SUB, D_MODEL), _F32)
        for c in range(D_FF // FF_CHUNK):
            cols = slice(c * FF_CHUNK, (c + 1) * FF_CHUNK)
            a = jnp.dot(h, w1_ref[0, :, cols], preferred_element_type=_F32)
            a = jnp.square(jnp.maximum(a, 0.0)).astype(_BF16)
            acc = acc + jnp.dot(a, w2_ref[0, cols, :], preferred_element_type=_F32)
        x2 = x1 + mod_ref[0, 5:6, :] * acc
        if final:
            ms = jnp.mean(x2 * x2, axis=-1, keepdims=True)
            x2 = x2 * lax.rsqrt(ms + EPS) * gf_ref[...]
        o_ref[0, rows, :] = x2


def _channel_mixing(x, mix, mod, gains2, gain_final, w_out, w_mlp1, w_mlp2, layer):
    B, S, _ = x.shape
    assert S % ROW_TILE == 0 and ROW_TILE % MLP_SUB == 0
    row = lambda b, i: (b, i, 0)
    this_layer = lambda b, i: (layer, 0, 0)
    resident = dict(pipeline_mode=pl.Buffered(1))
    return pl.pallas_call(
        functools.partial(_mlp_kernel, final=(layer == DEPTH - 1)),
        grid=(B, S // ROW_TILE),
        in_specs=[
            pl.BlockSpec((1, ROW_TILE, D_MODEL), row),
            pl.BlockSpec((1, ROW_TILE, D_MODEL), row),
            pl.BlockSpec((1, N_MOD, D_MODEL), lambda b, i: (b, 0, 0)),
            pl.BlockSpec((1, 1, D_MODEL), this_layer),
            pl.BlockSpec((1, D_MODEL), lambda b, i: (0, 0)),
            pl.BlockSpec((1, D_MODEL, D_MODEL), this_layer, **resident),
            pl.BlockSpec((1, D_MODEL, D_FF), this_layer, **resident),
            pl.BlockSpec((1, D_FF, D_MODEL), this_layer, **resident),
        ],
        out_specs=pl.BlockSpec((1, ROW_TILE, D_MODEL), row),
        out_shape=jax.ShapeDtypeStruct((B, S, D_MODEL), _F32),
        compiler_params=_params(2),
        name="channel_mixing",
    )(x, mix, mod, gains2, gain_final, w_out, w_mlp1, w_mlp2)


def _run_trunk(x, mod_rows, norm1_g, w_t, attn_sink, decays, w_out, norm2_g, w_mlp1, w_mlp2, final_g):
    B = x.shape[0]
    for l in range(DEPTH):
        mod = mod_rows[l].reshape(B, N_MOD, D_MODEL)
        k_nat, feat = _in_projection(x, mod, norm1_g, w_t, l)
        rf, rb = _retention_states(k_nat, feat, decays[l])
        mix = _token_mixing(k_nat, feat, rf, rb, attn_sink[l], decays[l])
        x = _channel_mixing(x, mix, mod, norm2_g, final_g, w_out, w_mlp1, w_mlp2, l)
    return x


def kernel(x_prompt, x_sample, c_prompt, c_sample, w_ada, b_ada, norm1_g, w_in, attn_sink,
           ret_decay_fwd, ret_decay_bwd, w_out, norm2_g, w_mlp1, w_mlp2, final_g):
    bp, bs = x_prompt.shape[0], x_sample.shape[0]
    c_all = jnp.concatenate(
        [c_prompt, c_sample, jnp.zeros((MOD_ROWS - bp - bs, D_MODEL), _F32)], axis=0)
    mod_all = _ada_modulation(c_all, w_ada, b_ada)
    decays = jnp.stack([jnp.repeat(ret_decay_fwd, RET_HEAD_DIM, axis=-1),
                        jnp.repeat(ret_decay_bwd, RET_HEAD_DIM, axis=-1)], axis=1)
    w_t = w_in.astype(_BF16).transpose(0, 2, 1)
    shared = (norm1_g.reshape(DEPTH, 1, D_MODEL), w_t, attn_sink, decays, w_out.astype(_BF16),
              norm2_g.reshape(DEPTH, 1, D_MODEL),
              w_mlp1.astype(_BF16), w_mlp2.astype(_BF16), final_g.reshape(1, D_MODEL))
    y_prompt = _run_trunk(x_prompt, mod_all[:, :bp], *shared)
    y_sample = _run_trunk(x_sample, mod_all[:, bp:bp + bs], *shared)
    return (y_prompt, y_sample)
```

```python
import functools

import jax
import jax.numpy as jnp
from jax import lax
from jax.experimental import pallas as pl
from jax.experimental.pallas import tpu as pltpu

D_MODEL = 1024
DEPTH = 4
ATTN_HEAD_DIM = 64
ATTN_WIDTH = 512
ATTN_HEADS = 8
ATTN_KV_HEADS = 2
ATTN_GROUP = 4
KV_WIDTH = 128
WINDOW = 128
BLOCK = 128
BAND = 3 * BLOCK
RET_WIDTH = 512
RET_HEAD_DIM = 128
RET_HEADS = 4
D_FF = 4 * D_MODEL
N_MOD = 6
EPS = 1e-6
NEG_INF = -1e30
LOG2_E = 1.4426950408889634
OFF_QA, OFF_KA, OFF_VA, OFF_QR, OFF_KR, OFF_VR, OFF_GR, IN_WIDTH = 0, 512, 640, 768, 1280, 1792, 2304, 2816

F_QA, F_QR, F_VR, F_GR, F_VA, F_ROWS = 0, 512, 1024, 1536, 2048, 2176
K_RET, K_ATT, K_COLS = 0, 512, 640

PROJ_TILE = 1024
PROJ_SUB = 512
ROW_TILE = 1024
MLP_SUB = 256
FF_CHUNK = 1024
ADA_COLS = 1536
MOD_ROWS = 16
MIX_BLOCKS = 8
STATE_UNROLL = 4
VMEM_LIMIT = 56 * 1024 * 1024

_BF16 = jnp.bfloat16
_F32 = jnp.float32
_BF16_ROWS = 16
_NT = (((1,), (1,)), ((), ()))


def _split_bf16(a):
    hi = a.astype(_BF16)
    lo = (a - hi.astype(_F32)).astype(_BF16)
    return hi, lo


def _params(n_axes):
    return pltpu.CompilerParams(
        dimension_semantics=("arbitrary",) * n_axes, vmem_limit_bytes=VMEM_LIMIT)


def _ada_kernel(c_ref, w_ref, b_ref, o_ref):
    c = c_ref[...]
    act = c * jax.nn.sigmoid(c)
    a_hi, a_lo = _split_bf16(act)
    w_hi, w_lo = _split_bf16(w_ref[0])
    acc = jnp.dot(a_hi, w_hi, preferred_element_type=_F32)
    acc += jnp.dot(a_hi, w_lo, preferred_element_type=_F32)
    acc += jnp.dot(a_lo, w_hi, preferred_element_type=_F32)
    o_ref[0] = acc + b_ref[0]


def _ada_modulation(c_all, w_ada, b_ada):
    n_cols = N_MOD * D_MODEL
    return pl.pallas_call(
        _ada_kernel,
        grid=(DEPTH, n_cols // ADA_COLS),
        in_specs=[
            pl.BlockSpec((MOD_ROWS, D_MODEL), lambda l, j: (0, 0)),
            pl.BlockSpec((1, D_MODEL, ADA_COLS), lambda l, j: (l, 0, j)),
            pl.BlockSpec((1, 1, ADA_COLS), lambda l, j: (l, 0, j)),
        ],
        out_specs=pl.BlockSpec((1, MOD_ROWS, ADA_COLS), lambda l, j: (l, 0, j)),
        out_shape=jax.ShapeDtypeStruct((DEPTH, MOD_ROWS, n_cols), _F32),
        compiler_params=_params(2),
        name="ada_modulation",
    )(c_all, w_ada, b_ada.reshape(DEPTH, 1, n_cols))


_W_GROUPS = (
    (OFF_QA, ATTN_WIDTH, ((F_QA, None),)),
    (OFF_KA, 2 * KV_WIDTH, ((None, K_ATT), (F_VA, None))),
    (OFF_QR, RET_WIDTH, ((F_QR, None),)),
    (OFF_KR, RET_WIDTH, ((None, K_RET),)),
    (OFF_VR, RET_WIDTH, ((F_VR, None),)),
    (OFF_GR, RET_WIDTH, ((F_GR, None),)),
)
Q_SCALE = ATTN_HEAD_DIM ** -0.5 * LOG2_E


def _modulated_norm(x, gain, shift, scale):
    ms = jnp.mean(x * x, axis=-1, keepdims=True)
    return (x * lax.rsqrt(ms + EPS) * gain) * (1.0 + scale) + shift


def _inproj_kernel(x_ref, mod_ref, g_ref, wt_ref, k_ref, f_ref):
    for t in range(PROJ_TILE // PROJ_SUB):
        rows = slice(t * PROJ_SUB, (t + 1) * PROJ_SUB)
        h = _modulated_norm(x_ref[0, rows, :], g_ref[0], mod_ref[0, 0:1, :], mod_ref[0, 1:2, :]).astype(_BF16)
        for w_off, w_rows, dests in _W_GROUPS:
            p = lax.dot_general(wt_ref[0, w_off:w_off + w_rows, :], h, _NT, preferred_element_type=_F32)
            if w_off == OFF_QA:
                p = p * Q_SCALE
            width = w_rows // len(dests)
            for i, (f_row, k_col) in enumerate(dests):
                part = p[i * width:(i + 1) * width, :]
                if f_row is None:
                    k_ref[0, rows, k_col:k_col + width] = part.T.astype(_BF16)
                else:
                    for c in range(PROJ_SUB // BLOCK):
                        f_ref[0, t * (PROJ_SUB // BLOCK) + c, f_row:f_row + width, :] = (
                            part[:, c * BLOCK:(c + 1) * BLOCK].astype(_BF16))


def _in_projection(x, mod, gains, w_t, layer):
    B, S, _ = x.shape
    assert S % PROJ_TILE == 0 and PROJ_TILE % PROJ_SUB == 0 and PROJ_SUB % BLOCK == 0
    blocks = PROJ_TILE // BLOCK
    this_layer = lambda b, i: (layer, 0, 0)
    return pl.pallas_call(
        _inproj_kernel,
        grid=(B, S // PROJ_TILE),
        in_specs=[
            pl.BlockSpec((1, PROJ_TILE, D_MODEL), lambda b, i: (b, i, 0)),
            pl.BlockSpec((1, N_MOD, D_MODEL), lambda b, i: (b, 0, 0)),
            pl.BlockSpec((1, 1, D_MODEL), this_layer),
            pl.BlockSpec((1, IN_WIDTH, D_MODEL), this_layer),
        ],
        out_specs=[
            pl.BlockSpec((1, PROJ_TILE, K_COLS), lambda b, i: (b, i, 0)),
            pl.BlockSpec((1, blocks, F_ROWS, BLOCK), lambda b, i: (b, i, 0, 0)),
        ],
        out_shape=[
            jax.ShapeDtypeStruct((B, S, K_COLS), _BF16),
            jax.ShapeDtypeStruct((B, S // BLOCK, F_ROWS, BLOCK), _BF16),
        ],
        compiler_params=_params(2),
        name="in_projection",
    )(x, mod, gains, w_t)


def _log_decay(decay):
    return jnp.minimum(decay, 0.0) - jnp.log1p(jnp.exp(-jnp.abs(decay)))


def _state_kernel(dec_ref, k_ref, v_ref, rf_ref, rb_ref, s_ref, kvb_ref, ktab_ref, ctab_ref, *, n_chunks):
    s_ref[...] = jnp.zeros_like(s_ref)
    pos = lax.broadcasted_iota(jnp.int32, (BLOCK, 1), 0).astype(_F32)
    for h in range(RET_HEADS):
        cols = slice(h * RET_HEAD_DIM, (h + 1) * RET_HEAD_DIM)
        for d in range(2):
            lg = _log_decay(dec_ref[d:d + 1, cols])
            expo = (BLOCK - 1.0 - pos) if d == 0 else pos
            ktab_ref[h, :, d * RET_HEAD_DIM:(d + 1) * RET_HEAD_DIM] = (RET_HEAD_DIM ** -0.5) * jnp.exp(expo * lg)
            ctab_ref[d, h] = jnp.broadcast_to(jnp.exp(float(BLOCK) * lg), (8, RET_HEAD_DIM))

    def forward(c, carry):
        row0 = pl.multiple_of(c * BLOCK, BLOCK)
        for h in range(RET_HEADS):
            cols = slice(h * RET_HEAD_DIM, (h + 1) * RET_HEAD_DIM)
            k_cols = slice(K_RET + h * RET_HEAD_DIM, K_RET + (h + 1) * RET_HEAD_DIM)
            k = k_ref[0, pl.ds(row0, BLOCK), k_cols].astype(_F32)
            k_dec = (jnp.concatenate([k, k], axis=1) * ktab_ref[h]).astype(_BF16)
            kv_t = jnp.dot(v_ref[0, c, cols, :], k_dec, preferred_element_type=_F32)
            state = s_ref[0, h]
            rf_ref[0, c, h] = state.astype(_BF16)
            s_ref[0, h] = state * ctab_ref[0, h, 0:1, :] + kv_t[:, :RET_HEAD_DIM]
            kvb_ref[c, h] = kv_t[:, RET_HEAD_DIM:]
        return carry

    def backward(i, carry):
        c = n_chunks - 1 - i
        for h in range(RET_HEADS):
            state = s_ref[1, h]
            rb_ref[0, c, h] = state.astype(_BF16)
            s_ref[1, h] = state * ctab_ref[1, h, 0:1, :] + kvb_ref[c, h]
        return carry

    lax.fori_loop(0, n_chunks, forward, 0, unroll=STATE_UNROLL)
    lax.fori_loop(0, n_chunks, backward, 0, unroll=STATE_UNROLL)


def _retention_states(k_nat, feat, decays):
    B, S, _ = k_nat.shape
    N = S // BLOCK
    assert N % STATE_UNROLL == 0
    st_shape = jax.ShapeDtypeStruct((B, N, RET_HEADS, RET_HEAD_DIM, RET_HEAD_DIM), _BF16)
    st_spec = pl.BlockSpec((1, N, RET_HEADS, RET_HEAD_DIM, RET_HEAD_DIM), lambda b: (b, 0, 0, 0, 0))
    return pl.pallas_call(
        functools.partial(_state_kernel, n_chunks=N),
        grid=(B,),
        in_specs=[
            pl.BlockSpec((2, RET_WIDTH), lambda b: (0, 0)),
            pl.BlockSpec((1, S, K_COLS), lambda b: (b, 0, 0)),
            pl.BlockSpec((1, N, RET_WIDTH, BLOCK), lambda b: (b, 0, F_VR // RET_WIDTH, 0)),
        ],
        out_specs=[st_spec, st_spec],
        out_shape=[st_shape, st_shape],
        scratch_shapes=[
            pltpu.VMEM((2, RET_HEADS, RET_HEAD_DIM, RET_HEAD_DIM), _F32),
            pltpu.VMEM((N, RET_HEADS, RET_HEAD_DIM, RET_HEAD_DIM), _F32),
            pltpu.VMEM((RET_HEADS, BLOCK, 2 * RET_HEAD_DIM), _F32),
            pltpu.VMEM((2, RET_HEADS, 8, RET_HEAD_DIM), _F32),
        ],
        compiler_params=_params(1),
        name="retention_states",
    )(decays, k_nat, feat)


def _mix_block(sub, variant, sink_ref, feat_ref, k_ref, k_blocks, v_blocks, rf_ref, rb_ref, o_ref,
               bias_ref, dcomb_ref, qdec_ref, s_ref, p_ref):
    tok = slice(sub * BLOCK, (sub + 1) * BLOCK)
    kband = jnp.concatenate(k_blocks[sub:sub + 3], axis=0)
    vband = jnp.concatenate(v_blocks[sub:sub + 3], axis=1)

    zeros = jnp.zeros((ATTN_HEAD_DIM, ATTN_GROUP * BLOCK), _BF16)
    q_rows = []
    for kh in range(ATTN_KV_HEADS):
        tiles = [feat_ref[0, sub, F_QA + hh * ATTN_HEAD_DIM:F_QA + (hh + 1) * ATTN_HEAD_DIM, :]
                 for hh in range(kh * ATTN_GROUP, (kh + 1) * ATTN_GROUP)]
        mine = jnp.concatenate(tiles, axis=1)
        q_rows.append(jnp.concatenate([mine, zeros] if kh == 0 else [zeros, mine], axis=1))
    q_bd = jnp.concatenate(q_rows, axis=0)
    scores = jnp.dot(kband, q_bd, preferred_element_type=_F32)
    for hh in range(ATTN_HEADS):
        s_ref[sub, hh] = scores[:, hh * BLOCK:(hh + 1) * BLOCK]

    ret_parts = {}

    def start_retention(h):
        k_cols = slice(K_RET + h * RET_HEAD_DIM, K_RET + (h + 1) * RET_HEAD_DIM)
        q_t = feat_ref[0, sub, F_QR + h * RET_HEAD_DIM:F_QR + (h + 1) * RET_HEAD_DIM, :]
        s_t = jnp.dot(k_ref[0, tok, k_cols], q_t, preferred_element_type=_F32)
        cross = jnp.dot(rf_ref[0, sub, h], q_t, preferred_element_type=_F32) * qdec_ref[0, h, 0:1, :]
        cross = cross + jnp.dot(rb_ref[0, sub, h], q_t, preferred_element_type=_F32) * qdec_ref[1, h, 0:1, :]
        ret_parts[h] = ((s_t * dcomb_ref[h]).astype(_BF16), cross)

    def finish_retention(h):
        a_t, cross = ret_parts[h]
        v_t = feat_ref[0, sub, F_VR + h * RET_HEAD_DIM:F_VR + (h + 1) * RET_HEAD_DIM, :]
        y = jnp.dot(v_t, a_t, preferred_element_type=_F32) + cross
        yc = y - jnp.mean(y, axis=0, keepdims=True)
        yn = yc * lax.rsqrt(jnp.mean(yc * yc, axis=0, keepdims=True) + EPS)
        gate = feat_ref[0, sub, F_GR + h * RET_HEAD_DIM:F_GR + (h + 1) * RET_HEAD_DIM, :].astype(_F32)
        lanes = slice(ATTN_WIDTH + h * RET_HEAD_DIM, ATTN_WIDTH + (h + 1) * RET_HEAD_DIM)
        o_ref[0, tok, lanes] = (gate * jax.nn.sigmoid(gate) * yn).T.astype(_BF16)

    sink_terms = []
    ones = jnp.ones((_BF16_ROWS, BAND), _BF16)

    def softmax_head(hh):
        st = s_ref[sub, hh] + bias_ref[variant, hh]
        sink = sink_ref[hh] * LOG2_E
        m = jnp.maximum(jnp.max(st, axis=0, keepdims=True), sink)
        p_ref[sub, hh] = jnp.exp2(st - m).astype(_BF16)
        sink_terms.append(jnp.exp2(sink - m))

    def attend(kh):
        v_t = jnp.concatenate([vband[kh * ATTN_HEAD_DIM:(kh + 1) * ATTN_HEAD_DIM, :], ones], axis=0)
        probs = jnp.concatenate(
            [p_ref[sub, hh] for hh in range(kh * ATTN_GROUP, (kh + 1) * ATTN_GROUP)], axis=1)
        o_t = jnp.dot(v_t, probs, preferred_element_type=_F32)
        for pair in range(ATTN_GROUP // 2):
            scaled = []
            for g in (2 * pair, 2 * pair + 1):
                q_lanes = slice(g * BLOCK, (g + 1) * BLOCK)
                denom = o_t[ATTN_HEAD_DIM:ATTN_HEAD_DIM + 1, q_lanes] + sink_terms[kh * ATTN_GROUP + g]
                scaled.append(o_t[:ATTN_HEAD_DIM, q_lanes] * (1.0 / denom))
            both = jnp.concatenate(scaled, axis=0)
            first = (kh * ATTN_GROUP + 2 * pair) * ATTN_HEAD_DIM
            o_ref[0, tok, first:first + 2 * ATTN_HEAD_DIM] = both.T.astype(_BF16)

    for hh in range(ATTN_HEADS):
        if hh % 2 == 0:
            start_retention(hh // 2)
        softmax_head(hh)
        if hh % 2 == 1:
            finish_retention(hh // 2)
        if hh == ATTN_GROUP - 1:
            attend(0)
    attend(1)


def _mix_kernel(sink_ref, dec_ref, feat_ref, k_ref, kp_ref, kn_ref, vp_ref, vn_ref, rf_ref, rb_ref, o_ref,
                bias_ref, dcomb_ref, qdec_ref, s_ref, p_ref, *, n_steps):
    step = pl.program_id(1)

    @pl.when(step == 0)
    def _():
        jj = lax.broadcasted_iota(jnp.int32, (BAND, BLOCK), 0)
        ii = lax.broadcasted_iota(jnp.int32, (BAND, BLOCK), 1)
        dist = jnp.abs(jj - BLOCK - ii)
        distf = dist.astype(_F32)
        has_prev, has_next = jj >= BLOCK, jj < 2 * BLOCK
        for hh in range(ATTN_HEADS):
            slope = 2.0 ** (-8.0 * (hh + 1.0) / ATTN_HEADS)
            table = jnp.where(dist <= WINDOW, (-slope * LOG2_E) * distf, NEG_INF)
            bias_ref[0, hh] = table
            bias_ref[1, hh] = jnp.where(has_prev, table, NEG_INF)
            bias_ref[2, hh] = jnp.where(has_next, table, NEG_INF)
        rj = lax.broadcasted_iota(jnp.int32, (BLOCK, BLOCK), 0)
        ri = lax.broadcasted_iota(jnp.int32, (BLOCK, BLOCK), 1)
        diff = (ri - rj).astype(_F32)
        posq = lax.broadcasted_iota(jnp.int32, (8, BLOCK), 1).astype(_F32)
        for h in range(RET_HEADS):
            cols = slice(h * RET_HEAD_DIM, (h + 1) * RET_HEAD_DIM)
            lg_f = _log_decay(dec_ref[0:1, cols])
            lg_b = _log_decay(dec_ref[1:2, cols])
            d_f = jnp.exp(jnp.maximum(diff, 0.0) * lg_f)
            d_b = jnp.exp(jnp.maximum(-diff, 0.0) * lg_b)
            dcomb_ref[h] = jnp.where(diff >= 0, d_f, d_b) * (RET_HEAD_DIM ** -0.5)
            qdec_ref[0, h] = jnp.exp((posq + 1.0) * lg_f)
            qdec_ref[1, h] = jnp.exp((float(BLOCK) - posq) * lg_b)

    k_blocks = ([kp_ref[0]]
                + [k_ref[0, i * BLOCK:(i + 1) * BLOCK, K_ATT:K_ATT + KV_WIDTH] for i in range(MIX_BLOCKS)]
                + [kn_ref[0]])
    v_blocks = ([vp_ref[0, 0]]
                + [feat_ref[0, i, F_VA:F_VA + KV_WIDTH, :] for i in range(MIX_BLOCKS)]
                + [vn_ref[0, 0]])
    for sub in range(MIX_BLOCKS):
        variant = 0
        if sub == 0:
            variant = jnp.where(step == 0, 1, variant)
        if sub == MIX_BLOCKS - 1:
            variant = jnp.where(step == n_steps - 1, 2, variant)
        _mix_block(sub, variant, sink_ref, feat_ref, k_ref, k_blocks, v_blocks, rf_ref, rb_ref, o_ref,
                   bias_ref, dcomb_ref, qdec_ref, s_ref, p_ref)


def _token_mixing(k_nat, feat, rf, rb, sink, decays):
    B, S, _ = k_nat.shape
    N = S // BLOCK
    n_steps = N // MIX_BLOCKS
    assert n_steps * MIX_BLOCKS == N and N >= 2
    prev_blk = lambda i: jnp.maximum(i * MIX_BLOCKS - 1, 0)
    next_blk = lambda i: jnp.minimum((i + 1) * MIX_BLOCKS, N - 1)
    k_col, v_row = K_ATT // KV_WIDTH, F_VA // KV_WIDTH
    st_spec = pl.BlockSpec((1, MIX_BLOCKS, RET_HEADS, RET_HEAD_DIM, RET_HEAD_DIM), lambda b, i: (b, i, 0, 0, 0))
    return pl.pallas_call(
        functools.partial(_mix_kernel, n_steps=n_steps),
        grid=(B, n_steps),
        in_specs=[
            pl.BlockSpec(memory_space=pltpu.SMEM),
            pl.BlockSpec((2, RET_WIDTH), lambda b, i: (0, 0)),
            pl.BlockSpec((1, MIX_BLOCKS, F_ROWS, BLOCK), lambda b, i: (b, i, 0, 0)),
            pl.BlockSpec((1, MIX_BLOCKS * BLOCK, K_COLS), lambda b, i: (b, i, 0)),
            pl.BlockSpec((1, BLOCK, KV_WIDTH), lambda b, i: (b, prev_blk(i), k_col)),
            pl.BlockSpec((1, BLOCK, KV_WIDTH), lambda b, i: (b, next_blk(i), k_col)),
            pl.BlockSpec((1, 1, KV_WIDTH, BLOCK), lambda b, i: (b, prev_blk(i), v_row, 0)),
            pl.BlockSpec((1, 1, KV_WIDTH, BLOCK), lambda b, i: (b, next_blk(i), v_row, 0)),
            st_spec, st_spec,
        ],
        out_specs=pl.BlockSpec((1, MIX_BLOCKS * BLOCK, D_MODEL), lambda b, i: (b, i, 0)),
        out_shape=jax.ShapeDtypeStruct((B, S, D_MODEL), _BF16),
        scratch_shapes=[
            pltpu.VMEM((3, ATTN_HEADS, BAND, BLOCK), _F32),
            pltpu.VMEM((RET_HEADS, BLOCK, BLOCK), _F32),
            pltpu.VMEM((2, RET_HEADS, 8, BLOCK), _F32),
            pltpu.VMEM((MIX_BLOCKS, ATTN_HEADS, BAND, BLOCK), _F32),
            pltpu.VMEM((MIX_BLOCKS, ATTN_HEADS, BAND, BLOCK), _BF16),
        ],
        compiler_params=_params(2),
        name="token_mixing",
    )(sink, decays, feat, k_nat, k_nat, k_nat, feat, feat, rf, rb)


def _mlp_kernel(x_ref, mix_ref, mod_ref, g2_ref, gf_ref, wo_ref, w1_ref, w2_ref, o_ref, *, final):
    subs = [slice(t * MLP_SUB, (t + 1) * MLP_SUB) for t in range(ROW_TILE // MLP_SUB)]
    projected = [jnp.dot(mix_ref[0, rows, :], wo_ref[0], preferred_element_type=_F32) for rows in subs]
    for rows, proj in zip(subs, projected):
        x1 = x_ref[0, rows, :] + mod_ref[0, 2:3, :] * proj
        h = _modulated_norm(x1, g2_ref[0], mod_ref[0, 3:4, :], mod_ref[0, 4:5, :]).astype(_BF16)
        acc = jnp.zeros((MLP_SUB, D_MODEL), _F32)
        for c in range(D_FF // FF_CHUNK):
            cols = slice(c * FF_CHUNK, (c + 1) * FF_CHUNK)
            a = jnp.dot(h, w1_ref[0, :, cols], preferred_element_type=_F32)
            a = jnp.square(jnp.maximum(a, 0.0)).astype(_BF16)
            acc = acc + jnp.dot(a, w2_ref[0, cols, :], preferred_element_type=_F32)
        x2 = x1 + mod_ref[0, 5:6, :] * acc
        if final:
            ms = jnp.mean(x2 * x2, axis=-1, keepdims=True)
            x2 = x2 * lax.rsqrt(ms + EPS) * gf_ref[...]
        o_ref[0, rows, :] = x2


def _channel_mixing(x, mix, mod, gains2, gain_final, w_out, w_mlp1, w_mlp2, layer):
    B, S, _ = x.shape
    assert S % ROW_TILE == 0 and ROW_TILE % MLP_SUB == 0
    row = lambda b, i: (b, i, 0)
    this_layer = lambda b, i: (layer, 0, 0)
    resident = dict(pipeline_mode=pl.Buffered(1))
    return pl.pallas_call(
        functools.partial(_mlp_kernel, final=(layer == DEPTH - 1)),
        grid=(B, S // ROW_TILE),
        in_specs=[
            pl.BlockSpec((1, ROW_TILE, D_MODEL), row),
            pl.BlockSpec((1, ROW_TILE, D_MODEL), row),
            pl.BlockSpec((1, N_MOD, D_MODEL), lambda b, i: (b, 0, 0)),
            pl.BlockSpec((1, 1, D_MODEL), this_layer),
            pl.BlockSpec((1, D_MODEL), lambda b, i: (0, 0)),
            pl.BlockSpec((1, D_MODEL, D_MODEL), this_layer, **resident),
            pl.BlockSpec((1, D_MODEL, D_FF), this_layer, **resident),
            pl.BlockSpec((1, D_FF, D_MODEL), this_layer, **resident),
        ],
        out_specs=pl.BlockSpec((1, ROW_TILE, D_MODEL), row),
        out_shape=jax.ShapeDtypeStruct((B, S, D_MODEL), _F32),
        compiler_params=_params(2),
        name="channel_mixing",
    )(x, mix, mod, gains2, gain_final, w_out, w_mlp1, w_mlp2)


def _run_trunk(x, mod_rows, norm1_g, w_t, attn_sink, decays, w_out, norm2_g, w_mlp1, w_mlp2, final_g):
    B = x.shape[0]
    for l in range(DEPTH):
        mod = mod_rows[l].reshape(B, N_MOD, D_MODEL)
        k_nat, feat = _in_projection(x, mod, norm1_g, w_t, l)
        rf, rb = _retention_states(k_nat, feat, decays[l])
        mix = _token_mixing(k_nat, feat, rf, rb, attn_sink[l], decays[l])
        x = _channel_mixing(x, mix, mod, norm2_g, final_g, w_out, w_mlp1, w_mlp2, l)
    return x


def kernel(x_prompt, x_sample, c_prompt, c_sample, w_ada, b_ada, norm1_g, w_in, attn_sink,
           ret_decay_fwd, ret_decay_bwd, w_out, norm2_g, w_mlp1, w_mlp2, final_g):
    bp, bs = x_prompt.shape[0], x_sample.shape[0]
    c_all = jnp.concatenate(
        [c_prompt, c_sample, jnp.zeros((MOD_ROWS - bp - bs, D_MODEL), _F32)], axis=0)
    mod_all = _ada_modulation(c_all, w_ada, b_ada)
    decays = jnp.stack([jnp.repeat(ret_decay_fwd, RET_HEAD_DIM, axis=-1),
                        jnp.repeat(ret_decay_bwd, RET_HEAD_DIM, axis=-1)], axis=1)
    w_t = w_in.astype(_BF16).transpose(0, 2, 1)
    shared = (norm1_g.reshape(DEPTH, 1, D_MODEL), w_t, attn_sink, decays, w_out.astype(_BF16),
              norm2_g.reshape(DEPTH, 1, D_MODEL),
              w_mlp1.astype(_BF16), w_mlp2.astype(_BF16), final_g.reshape(1, D_MODEL))
    y_prompt = _run_trunk(x_prompt, mod_all[:, :bp], *shared)
    y_sample = _run_trunk(x_sample, mod_all[:, bp:bp + bs], *shared)
    return (y_prompt, y_sample)
```

```python
import functools

import jax
import jax.numpy as jnp
from jax import lax
from jax.experimental import pallas as pl
from jax.experimental.pallas import tpu as pltpu

D_MODEL = 1024
DEPTH = 4
ATTN_HEAD_DIM = 64
ATTN_WIDTH = 512
ATTN_HEADS = 8
ATTN_KV_HEADS = 2
ATTN_GROUP = 4
KV_WIDTH = 128
WINDOW = 128
BLOCK = 128
BAND = 3 * BLOCK
RET_WIDTH = 512
RET_HEAD_DIM = 128
RET_HEADS = 4
D_FF = 4 * D_MODEL
N_MOD = 6
EPS = 1e-6
NEG_INF = -1e30
LOG2_E = 1.4426950408889634
OFF_QA, OFF_KA, OFF_VA, OFF_QR, OFF_KR, OFF_VR, OFF_GR, IN_WIDTH = 0, 512, 640, 768, 1280, 1792, 2304, 2816

F_QA, F_QR, F_VR, F_GR, F_VA, F_ROWS = 0, 512, 1024, 1536, 2048, 2176
K_RET, K_ATT, K_COLS = 0, 512, 640

PROJ_TILE = 1024
PROJ_SUB = 512
ROW_TILE = 1024
MLP_SUB = 256
FF_CHUNK = 1024
ADA_COLS = 1536
MOD_ROWS = 16
MIX_BLOCKS = 8
STATE_UNROLL = 4
VMEM_LIMIT = 56 * 1024 * 1024

_BF16 = jnp.bfloat16
_F32 = jnp.float32
_BF16_ROWS = 16
_NT = (((1,), (1,)), ((), ()))


def _split_bf16(a):
    hi = a.astype(_BF16)
    lo = (a - hi.astype(_F32)).astype(_BF16)
    return hi, lo


def _params(n_axes):
    return pltpu.CompilerParams(
        dimension_semantics=("arbitrary",) * n_axes, vmem_limit_bytes=VMEM_LIMIT)


def _ada_kernel(c_ref, w_ref, b_ref, o_ref):
    c = c_ref[...]
    act = c * jax.nn.sigmoid(c)
    a_hi, a_lo = _split_bf16(act)
    w_hi, w_lo = _split_bf16(w_ref[0])
    acc = jnp.dot(a_hi, w_hi, preferred_element_type=_F32)
    acc += jnp.dot(a_hi, w_lo, preferred_element_type=_F32)
    acc += jnp.dot(a_lo, w_hi, preferred_element_type=_F32)
    o_ref[0] = acc + b_ref[0]


def _ada_modulation(c_all, w_ada, b_ada):
    n_cols = N_MOD * D_MODEL
    return pl.pallas_call(
        _ada_kernel,
        grid=(DEPTH, n_cols // ADA_COLS),
        in_specs=[
            pl.BlockSpec((MOD_ROWS, D_MODEL), lambda l, j: (0, 0)),
            pl.BlockSpec((1, D_MODEL, ADA_COLS), lambda l, j: (l, 0, j)),
            pl.BlockSpec((1, 1, ADA_COLS), lambda l, j: (l, 0, j)),
        ],
        out_specs=pl.BlockSpec((1, MOD_ROWS, ADA_COLS), lambda l, j: (l, 0, j)),
        out_shape=jax.ShapeDtypeStruct((DEPTH, MOD_ROWS, n_cols), _F32),
        compiler_params=_params(2),
        name="ada_modulation",
    )(c_all, w_ada, b_ada.reshape(DEPTH, 1, n_cols))


_W_GROUPS = (
    (OFF_QA, ATTN_WIDTH, ((F_QA, None),)),
    (OFF_KA, 2 * KV_WIDTH, ((None, K_ATT), (F_VA, None))),
    (OFF_QR, RET_WIDTH, ((F_QR, None),)),
    (OFF_KR, RET_WIDTH, ((None, K_RET),)),
    (OFF_VR, RET_WIDTH, ((F_VR, None),)),
    (OFF_GR, RET_WIDTH, ((F_GR, None),)),
)
Q_SCALE = ATTN_HEAD_DIM ** -0.5 * LOG2_E


def _modulated_norm(x, gain, shift, scale):
    ms = jnp.mean(x * x, axis=-1, keepdims=True)
    return (x * lax.rsqrt(ms + EPS) * gain) * (1.0 + scale) + shift


def _inproj_kernel(x_ref, mod_ref, g_ref, wt_ref, k_ref, f_ref):
    for t in range(PROJ_TILE // PROJ_SUB):
        rows = slice(t * PROJ_SUB, (t + 1) * PROJ_SUB)
        h = _modulated_norm(x_ref[0, rows, :], g_ref[0], mod_ref[0, 0:1, :], mod_ref[0, 1:2, :]).astype(_BF16)
        for w_off, w_rows, dests in _W_GROUPS:
            p = lax.dot_general(wt_ref[0, w_off:w_off + w_rows, :], h, _NT, preferred_element_type=_F32)
            if w_off == OFF_QA:
                p = p * Q_SCALE
            width = w_rows // len(dests)
            for i, (f_row, k_col) in enumerate(dests):
                part = p[i * width:(i + 1) * width, :]
                if f_row is None:
                    k_ref[0, rows, k_col:k_col + width] = part.T.astype(_BF16)
                else:
                    for c in range(PROJ_SUB // BLOCK):
                        f_ref[0, t * (PROJ_SUB // BLOCK) + c, f_row:f_row + width, :] = (
                            part[:, c * BLOCK:(c + 1) * BLOCK].astype(_BF16))


def _in_projection(x, mod, gains, w_t, layer):
    B, S, _ = x.shape
    assert S % PROJ_TILE == 0 and PROJ_TILE % PROJ_SUB == 0 and PROJ_SUB % BLOCK == 0
    blocks = PROJ_TILE // BLOCK
    this_layer = lambda b, i: (layer, 0, 0)
    return pl.pallas_call(
        _inproj_kernel,
        grid=(B, S // PROJ_TILE),
        in_specs=[
            pl.BlockSpec((1, PROJ_TILE, D_MODEL), lambda b, i: (b, i, 0)),
            pl.BlockSpec((1, N_MOD, D_MODEL), lambda b, i: (b, 0, 0)),
            pl.BlockSpec((1, 1, D_MODEL), this_layer),
            pl.BlockSpec((1, IN_WIDTH, D_MODEL), this_layer),
        ],
        out_specs=[
            pl.BlockSpec((1, PROJ_TILE, K_COLS), lambda b, i: (b, i, 0)),
            pl.BlockSpec((1, blocks, F_ROWS, BLOCK), lambda b, i: (b, i, 0, 0)),
        ],
        out_shape=[
            jax.ShapeDtypeStruct((B, S, K_COLS), _BF16),
            jax.ShapeDtypeStruct((B, S // BLOCK, F_ROWS, BLOCK), _BF16),
        ],
        compiler_params=_params(2),
        name="in_projection",
    )(x, mod, gains, w_t)


def _log_decay(decay):
    return jnp.minimum(decay, 0.0) - jnp.log1p(jnp.exp(-jnp.abs(decay)))


def _state_kernel(dec_ref, k_ref, v_ref, rf_ref, rb_ref, s_ref, kvb_ref, ktab_ref, ctab_ref, *, n_chunks):
    s_ref[...] = jnp.zeros_like(s_ref)
    pos = lax.broadcasted_iota(jnp.int32, (BLOCK, 1), 0).astype(_F32)
    for h in range(RET_HEADS):
        cols = slice(h * RET_HEAD_DIM, (h + 1) * RET_HEAD_DIM)
        for d in range(2):
            lg = _log_decay(dec_ref[d:d + 1, cols])
            expo = (BLOCK - 1.0 - pos) if d == 0 else pos
            ktab_ref[h, :, d * RET_HEAD_DIM:(d + 1) * RET_HEAD_DIM] = (RET_HEAD_DIM ** -0.5) * jnp.exp(expo * lg)
            ctab_ref[d, h] = jnp.broadcast_to(jnp.exp(float(BLOCK) * lg), (8, RET_HEAD_DIM))

    def forward(c, carry):
        row0 = pl.multiple_of(c * BLOCK, BLOCK)
        for h in range(RET_HEADS):
            cols = slice(h * RET_HEAD_DIM, (h + 1) * RET_HEAD_DIM)
            k_cols = slice(K_RET + h * RET_HEAD_DIM, K_RET + (h + 1) * RET_HEAD_DIM)
            k = k_ref[0, pl.ds(row0, BLOCK), k_cols].astype(_F32)
            k_dec = (jnp.concatenate([k, k], axis=1) * ktab_ref[h]).astype(_BF16)
            kv_t = jnp.dot(v_ref[0, c, cols, :], k_dec, preferred_element_type=_F32)
            state = s_ref[0, h]
            rf_ref[0, c, h] = state.astype(_BF16)
            s_ref[0, h] = state * ctab_ref[0, h, 0:1, :] + kv_t[:, :RET_HEAD_DIM]
            kvb_ref[c, h] = kv_t[:, RET_HEAD_DIM:]
        return carry

    def backward(i, carry):
        c = n_chunks - 1 - i
        for h in range(RET_HEADS):
            state = s_ref[1, h]
            rb_ref[0, c, h] = state.astype(_BF16)
            s_ref[1, h] = state * ctab_ref[1, h, 0:1, :] + kvb_ref[c, h]
        return carry

    lax.fori_loop(0, n_chunks, forward, 0, unroll=STATE_UNROLL)
    lax.fori_loop(0, n_chunks, backward, 0, unroll=STATE_UNROLL)


def _retention_states(k_nat, feat, decays):
    B, S, _ = k_nat.shape
    N = S // BLOCK
    assert N % STATE_UNROLL == 0
    st_shape = jax.ShapeDtypeStruct((B, N, RET_HEADS, RET_HEAD_DIM, RET_HEAD_DIM), _BF16)
    st_spec = pl.BlockSpec((1, N, RET_HEADS, RET_HEAD_DIM, RET_HEAD_DIM), lambda b: (b, 0, 0, 0, 0))
    return pl.pallas_call(
        functools.partial(_state_kernel, n_chunks=N),
        grid=(B,),
        in_specs=[
            pl.BlockSpec((2, RET_WIDTH), lambda b: (0, 0)),
            pl.BlockSpec((1, S, K_COLS), lambda b: (b, 0, 0)),
            pl.BlockSpec((1, N, RET_WIDTH, BLOCK), lambda b: (b, 0, F_VR // RET_WIDTH, 0)),
        ],
        out_specs=[st_spec, st_spec],
        out_shape=[st_shape, st_shape],
        scratch_shapes=[
            pltpu.VMEM((2, RET_HEADS, RET_HEAD_DIM, RET_HEAD_DIM), _F32),
            pltpu.VMEM((N, RET_HEADS, RET_HEAD_DIM, RET_HEAD_DIM), _F32),
            pltpu.VMEM((RET_HEADS, BLOCK, 2 * RET_HEAD_DIM), _F32),
            pltpu.VMEM((2, RET_HEADS, 8, RET_HEAD_DIM), _F32),
        ],
        compiler_params=_params(1),
        name="retention_states",
    )(decays, k_nat, feat)


def _mix_block(sub, variant, sink_ref, feat_ref, k_ref, k_blocks, v_blocks, rf_ref, rb_ref, o_ref,
               bias_ref, dcomb_ref, qdec_ref, s_ref, p_ref, out_ref):
    tok = slice(sub * BLOCK, (sub + 1) * BLOCK)
    kband = jnp.concatenate(k_blocks[sub:sub + 3], axis=0)
    vband = jnp.concatenate(v_blocks[sub:sub + 3], axis=1)

    zeros = jnp.zeros((ATTN_HEAD_DIM, ATTN_GROUP * BLOCK), _BF16)
    q_rows = []
    for kh in range(ATTN_KV_HEADS):
        tiles = [feat_ref[0, sub, F_QA + hh * ATTN_HEAD_DIM:F_QA + (hh + 1) * ATTN_HEAD_DIM, :]
                 for hh in range(kh * ATTN_GROUP, (kh + 1) * ATTN_GROUP)]
        mine = jnp.concatenate(tiles, axis=1)
        q_rows.append(jnp.concatenate([mine, zeros] if kh == 0 else [zeros, mine], axis=1))
    q_bd = jnp.concatenate(q_rows, axis=0)
    scores = jnp.dot(kband, q_bd, preferred_element_type=_F32)
    for hh in range(ATTN_HEADS):
        s_ref[sub, hh] = scores[:, hh * BLOCK:(hh + 1) * BLOCK]

    ret_parts = {}

    def start_retention(h):
        k_cols = slice(K_RET + h * RET_HEAD_DIM, K_RET + (h + 1) * RET_HEAD_DIM)
        q_t = feat_ref[0, sub, F_QR + h * RET_HEAD_DIM:F_QR + (h + 1) * RET_HEAD_DIM, :]
        s_t = jnp.dot(k_ref[0, tok, k_cols], q_t, preferred_element_type=_F32)
        cross = jnp.dot(rf_ref[0, sub, h], q_t, preferred_element_type=_F32) * qdec_ref[0, h, 0:1, :]
        cross = cross + jnp.dot(rb_ref[0, sub, h], q_t, preferred_element_type=_F32) * qdec_ref[1, h, 0:1, :]
        ret_parts[h] = ((s_t * dcomb_ref[h]).astype(_BF16), cross)

    def finish_retention(h):
        a_t, cross = ret_parts[h]
        v_t = feat_ref[0, sub, F_VR + h * RET_HEAD_DIM:F_VR + (h + 1) * RET_HEAD_DIM, :]
        y = jnp.dot(v_t, a_t, preferred_element_type=_F32) + cross
        yc = y - jnp.mean(y, axis=0, keepdims=True)
        yn = yc * lax.rsqrt(jnp.mean(yc * yc, axis=0, keepdims=True) + EPS)
        gate = feat_ref[0, sub, F_GR + h * RET_HEAD_DIM:F_GR + (h + 1) * RET_HEAD_DIM, :].astype(_F32)
        out_ref[sub, ATTN_WIDTH // BLOCK + h] = (gate * jax.nn.sigmoid(gate) * yn).T.astype(_BF16)

    sink_terms = []
    ones = jnp.ones((_BF16_ROWS, BAND), _BF16)

    def softmax_head(hh):
        st = s_ref[sub, hh] + bias_ref[variant, hh]
        sink = sink_ref[hh] * LOG2_E
        m = jnp.maximum(jnp.max(st, axis=0, keepdims=True), sink)
        p_ref[sub, hh] = jnp.exp2(st - m).astype(_BF16)
        sink_terms.append(jnp.exp2(sink - m))

    def attend(kh):
        v_t = jnp.concatenate([vband[kh * ATTN_HEAD_DIM:(kh + 1) * ATTN_HEAD_DIM, :], ones], axis=0)
        probs = jnp.concatenate(
            [p_ref[sub, hh] for hh in range(kh * ATTN_GROUP, (kh + 1) * ATTN_GROUP)], axis=1)
        o_t = jnp.dot(v_t, probs, preferred_element_type=_F32)
        for pair in range(ATTN_GROUP // 2):
            scaled = []
            for g in (2 * pair, 2 * pair + 1):
                q_lanes = slice(g * BLOCK, (g + 1) * BLOCK)
                denom = o_t[ATTN_HEAD_DIM:ATTN_HEAD_DIM + 1, q_lanes] + sink_terms[kh * ATTN_GROUP + g]
                scaled.append(o_t[:ATTN_HEAD_DIM, q_lanes] * (1.0 / denom))
            both = jnp.concatenate(scaled, axis=0)
            out_ref[sub, kh * (ATTN_GROUP // 2) + pair] = both.T.astype(_BF16)

    for hh in range(ATTN_HEADS):
        if hh % 2 == 0:
            start_retention(hh // 2)
        softmax_head(hh)
        if hh % 2 == 1:
            finish_retention(hh // 2)
        if hh == ATTN_GROUP - 1:
            attend(0)
    attend(1)
    o_ref[0, tok, :] = jnp.concatenate([out_ref[sub, t] for t in range(D_MODEL // BLOCK)], axis=1)


def _mix_kernel(sink_ref, dec_ref, feat_ref, k_ref, kp_ref, kn_ref, vp_ref, vn_ref, rf_ref, rb_ref, o_ref,
                bias_ref, dcomb_ref, qdec_ref, s_ref, p_ref, out_ref, *, n_steps):
    step = pl.program_id(1)

    @pl.when(step == 0)
    def _():
        jj = lax.broadcasted_iota(jnp.int32, (BAND, BLOCK), 0)
        ii = lax.broadcasted_iota(jnp.int32, (BAND, BLOCK), 1)
        dist = jnp.abs(jj - BLOCK - ii)
        distf = dist.astype(_F32)
        has_prev, has_next = jj >= BLOCK, jj < 2 * BLOCK
        for hh in range(ATTN_HEADS):
            slope = 2.0 ** (-8.0 * (hh + 1.0) / ATTN_HEADS)
            table = jnp.where(dist <= WINDOW, (-slope * LOG2_E) * distf, NEG_INF)
            bias_ref[0, hh] = table
            bias_ref[1, hh] = jnp.where(has_prev, table, NEG_INF)
            bias_ref[2, hh] = jnp.where(has_next, table, NEG_INF)
        rj = lax.broadcasted_iota(jnp.int32, (BLOCK, BLOCK), 0)
        ri = lax.broadcasted_iota(jnp.int32, (BLOCK, BLOCK), 1)
        diff = (ri - rj).astype(_F32)
        posq = lax.broadcasted_iota(jnp.int32, (8, BLOCK), 1).astype(_F32)
        for h in range(RET_HEADS):
            cols = slice(h * RET_HEAD_DIM, (h + 1) * RET_HEAD_DIM)
            lg_f = _log_decay(dec_ref[0:1, cols])
            lg_b = _log_decay(dec_ref[1:2, cols])
            d_f = jnp.exp(jnp.maximum(diff, 0.0) * lg_f)
            d_b = jnp.exp(jnp.maximum(-diff, 0.0) * lg_b)
            dcomb_ref[h] = jnp.where(diff >= 0, d_f, d_b) * (RET_HEAD_DIM ** -0.5)
            qdec_ref[0, h] = jnp.exp((posq + 1.0) * lg_f)
            qdec_ref[1, h] = jnp.exp((float(BLOCK) - posq) * lg_b)

    k_blocks = ([kp_ref[0]]
                + [k_ref[0, i * BLOCK:(i + 1) * BLOCK, K_ATT:K_ATT + KV_WIDTH] for i in range(MIX_BLOCKS)]
                + [kn_ref[0]])
    v_blocks = ([vp_ref[0, 0]]
                + [feat_ref[0, i, F_VA:F_VA + KV_WIDTH, :] for i in range(MIX_BLOCKS)]
                + [vn_ref[0, 0]])
    for sub in range(MIX_BLOCKS):
        variant = 0
        if sub == 0:
            variant = jnp.where(step == 0, 1, variant)
        if sub == MIX_BLOCKS - 1:
            variant = jnp.where(step == n_steps - 1, 2, variant)
        _mix_block(sub, variant, sink_ref, feat_ref, k_ref, k_blocks, v_blocks, rf_ref, rb_ref, o_ref,
                   bias_ref, dcomb_ref, qdec_ref, s_ref, p_ref, out_ref)


def _token_mixing(k_nat, feat, rf, rb, sink, decays):
    B, S, _ = k_nat.shape
    N = S // BLOCK
    n_steps = N // MIX_BLOCKS
    assert n_steps * MIX_BLOCKS == N and N >= 2
    prev_blk = lambda i: jnp.maximum(i * MIX_BLOCKS - 1, 0)
    next_blk = lambda i: jnp.minimum((i + 1) * MIX_BLOCKS, N - 1)
    k_col, v_row = K_ATT // KV_WIDTH, F_VA // KV_WIDTH
    st_spec = pl.BlockSpec((1, MIX_BLOCKS, RET_HEADS, RET_HEAD_DIM, RET_HEAD_DIM), lambda b, i: (b, i, 0, 0, 0))
    return pl.pallas_call(
        functools.partial(_mix_kernel, n_steps=n_steps),
        grid=(B, n_steps),
        in_specs=[
            pl.BlockSpec(memory_space=pltpu.SMEM),
            pl.BlockSpec((2, RET_WIDTH), lambda b, i: (0, 0)),
            pl.BlockSpec((1, MIX_BLOCKS, F_ROWS, BLOCK), lambda b, i: (b, i, 0, 0)),
            pl.BlockSpec((1, MIX_BLOCKS * BLOCK, K_COLS), lambda b, i: (b, i, 0)),
            pl.BlockSpec((1, BLOCK, KV_WIDTH), lambda b, i: (b, prev_blk(i), k_col)),
            pl.BlockSpec((1, BLOCK, KV_WIDTH), lambda b, i: (b, next_blk(i), k_col)),
            pl.BlockSpec((1, 1, KV_WIDTH, BLOCK), lambda b, i: (b, prev_blk(i), v_row, 0)),
            pl.BlockSpec((1, 1, KV_WIDTH, BLOCK), lambda b, i: (b, next_blk(i), v_row, 0)),
            st_spec, st_spec,
        ],
        out_specs=pl.BlockSpec((1, MIX_BLOCKS * BLOCK, D_MODEL), lambda b, i: (b, i, 0)),
        out_shape=jax.ShapeDtypeStruct((B, S, D_MODEL), _BF16),
        scratch_shapes=[
            pltpu.VMEM((3, ATTN_HEADS, BAND, BLOCK), _F32),
            pltpu.VMEM((RET_HEADS, BLOCK, BLOCK), _F32),
            pltpu.VMEM((2, RET_HEADS, 8, BLOCK), _F32),
            pltpu.VMEM((MIX_BLOCKS, ATTN_HEADS, BAND, BLOCK), _F32),
            pltpu.VMEM((MIX_BLOCKS, ATTN_HEADS, BAND, BLOCK), _BF16),
            pltpu.VMEM((MIX_BLOCKS, D_MODEL // BLOCK, BLOCK, BLOCK), _BF16),
        ],
        compiler_params=_params(2),
        name="token_mixing",
    )(sink, decays, feat, k_nat, k_nat, k_nat, feat, feat, rf, rb)


def _mlp_kernel(x_ref, mix_ref, mod_ref, g2_ref, gf_ref, wo_ref, w1_ref, w2_ref, o_ref, *, final):
    subs = [slice(t * MLP_SUB, (t + 1) * MLP_SUB) for t in range(ROW_TILE // MLP_SUB)]
    projected = [jnp.dot(mix_ref[0, rows, :], wo_ref[0], preferred_element_type=_F32) for rows in subs]
    for rows, proj in zip(subs, projected):
        x1 = x_ref[0, rows, :] + mod_ref[0, 2:3, :] * proj
        h = _modulated_norm(x1, g2_ref[0], mod_ref[0, 3:4, :], mod_ref[0, 4:5, :]).astype(_BF16)
        acc = jnp.zeros((MLP_SUB, D_MODEL), _F32)
        for c in range(D_FF // FF_CHUNK):
            cols = slice(c * FF_CHUNK, (c + 1) * FF_CHUNK)
            a = jnp.dot(h, w1_ref[0, :, cols], preferred_element_type=_F32)
            a = jnp.square(jnp.maximum(a, 0.0)).astype(_BF16)
            acc = acc + jnp.dot(a, w2_ref[0, cols, :], preferred_element_type=_F32)
        x2 = x1 + mod_ref[0, 5:6, :] * acc
        if final:
            ms = jnp.mean(x2 * x2, axis=-1, keepdims=True)
            x2 = x2 * lax.rsqrt(ms + EPS) * gf_ref[...]
        o_ref[0, rows, :] = x2


def _channel_mixing(x, mix, mod, gains2, gain_final, w_out, w_mlp1, w_mlp2, layer):
    B, S, _ = x.shape
    assert S % ROW_TILE == 0 and ROW_TILE % MLP_SUB == 0
    row = lambda b, i: (b, i, 0)
    this_layer = lambda b, i: (layer, 0, 0)
    resident = dict(pipeline_mode=pl.Buffered(1))
    return pl.pallas_call(
        functools.partial(_mlp_kernel, final=(layer == DEPTH - 1)),
        grid=(B, S // ROW_TILE),
        in_specs=[
            pl.BlockSpec((1, ROW_TILE, D_MODEL), row),
            pl.BlockSpec((1, ROW_TILE, D_MODEL), row),
            pl.BlockSpec((1, N_MOD, D_MODEL), lambda b, i: (b, 0, 0)),
            pl.BlockSpec((1, 1, D_MODEL), this_layer),
            pl.BlockSpec((1, D_MODEL), lambda b, i: (0, 0)),
            pl.BlockSpec((1, D_MODEL, D_MODEL), this_layer, **resident),
            pl.BlockSpec((1, D_MODEL, D_FF), this_layer, **resident),
            pl.BlockSpec((1, D_FF, D_MODEL), this_layer, **resident),
        ],
        out_specs=pl.BlockSpec((1, ROW_TILE, D_MODEL), row),
        out_shape=jax.ShapeDtypeStruct((B, S, D_MODEL), _F32),
        compiler_params=_params(2),
        name="channel_mixing",
    )(x, mix, mod, gains2, gain_final, w_out, w_mlp1, w_mlp2)


def _run_trunk(x, mod_rows, norm1_g, w_t, attn_sink, decays, w_out, norm2_g, w_mlp1, w_mlp2, final_g):
    B = x.shape[0]
    for l in range(DEPTH):
        mod = mod_rows[l].reshape(B, N_MOD, D_MODEL)
        k_nat, feat = _in_projection(x, mod, norm1_g, w_t, l)
        rf, rb = _retention_states(k_nat, feat, decays[l])
        mix = _token_mixing(k_nat, feat, rf, rb, attn_sink[l], decays[l])
        x = _channel_mixing(x, mix, mod, norm2_g, final_g, w_out, w_mlp1, w_mlp2, l)
    return x


def kernel(x_prompt, x_sample, c_prompt, c_sample, w_ada, b_ada, norm1_g, w_in, attn_sink,
           ret_decay_fwd, ret_decay_bwd, w_out, norm2_g, w_mlp1, w_mlp2, final_g):
    bp, bs = x_prompt.shape[0], x_sample.shape[0]
    c_all = jnp.concatenate(
        [c_prompt, c_sample, jnp.zeros((MOD_ROWS - bp - bs, D_MODEL), _F32)], axis=0)
    mod_all = _ada_modulation(c_all, w_ada, b_ada)
    decays = jnp.stack([jnp.repeat(ret_decay_fwd, RET_HEAD_DIM, axis=-1),
                        jnp.repeat(ret_decay_bwd, RET_HEAD_DIM, axis=-1)], axis=1)
    w_t = w_in.astype(_BF16).transpose(0, 2, 1)
    shared = (norm1_g.reshape(DEPTH, 1, D_MODEL), w_t, attn_sink, decays, w_out.astype(_BF16),
              norm2_g.reshape(DEPTH, 1, D_MODEL),
              w_mlp1.astype(_BF16), w_mlp2.astype(_BF16), final_g.reshape(1, D_MODEL))
    y_prompt = _run_trunk(x_prompt, mod_all[:, :bp], *shared)
    y_sample = _run_trunk(x_sample, mod_all[:, bp:bp + bs], *shared)
    return (y_prompt, y_sample)
```

```python
import functools

import jax
import jax.numpy as jnp
from jax import lax
from jax.experimental import pallas as pl
from jax.experimental.pallas import tpu as pltpu

D_MODEL = 1024
DEPTH = 4
ATTN_HEAD_DIM = 64
ATTN_WIDTH = 512
ATTN_HEADS = 8
ATTN_KV_HEADS = 2
ATTN_GROUP = 4
KV_WIDTH = 128
WINDOW = 128
BLOCK = 128
BAND = 3 * BLOCK
RET_WIDTH = 512
RET_HEAD_DIM = 128
RET_HEADS = 4
D_FF = 4 * D_MODEL
N_MOD = 6
EPS = 1e-6
NEG_INF = -1e30
LOG2_E = 1.4426950408889634
OFF_QA, OFF_KA, OFF_VA, OFF_QR, OFF_KR, OFF_VR, OFF_GR, IN_WIDTH = 0, 512, 640, 768, 1280, 1792, 2304, 2816

F_QA, F_QR, F_VR, F_GR, F_VA, F_ROWS = 0, 512, 1024, 1536, 2048, 2176
K_RET, K_ATT, K_COLS = 0, 512, 640

PROJ_TILE = 1024
PROJ_SUB = 512
ROW_TILE = 1024
MLP_SUB = 512
FF_CHUNK = 1024
ADA_COLS = 1536
MOD_ROWS = 16
MIX_BLOCKS = 8
STATE_UNROLL = 4
VMEM_LIMIT = 56 * 1024 * 1024

_BF16 = jnp.bfloat16
_F32 = jnp.float32
_BF16_ROWS = 16
_NT = (((1,), (1,)), ((), ()))


def _split_bf16(a):
    hi = a.astype(_BF16)
    lo = (a - hi.astype(_F32)).astype(_BF16)
    return hi, lo


def _params(n_axes):
    return pltpu.CompilerParams(
        dimension_semantics=("arbitrary",) * n_axes, vmem_limit_bytes=VMEM_LIMIT)


def _ada_kernel(c_ref, w_ref, b_ref, o_ref):
    c = c_ref[...]
    act = c * jax.nn.sigmoid(c)
    a_hi, a_lo = _split_bf16(act)
    w_hi, w_lo = _split_bf16(w_ref[0])
    acc = jnp.dot(a_hi, w_hi, preferred_element_type=_F32)
    acc += jnp.dot(a_hi, w_lo, preferred_element_type=_F32)
    acc += jnp.dot(a_lo, w_hi, preferred_element_type=_F32)
    o_ref[0] = acc + b_ref[0]


def _ada_modulation(c_all, w_ada, b_ada):
    n_cols = N_MOD * D_MODEL
    return pl.pallas_call(
        _ada_kernel,
        grid=(DEPTH, n_cols // ADA_COLS),
        in_specs=[
            pl.BlockSpec((MOD_ROWS, D_MODEL), lambda l, j: (0, 0)),
            pl.BlockSpec((1, D_MODEL, ADA_COLS), lambda l, j: (l, 0, j)),
            pl.BlockSpec((1, 1, ADA_COLS), lambda l, j: (l, 0, j)),
        ],
        out_specs=pl.BlockSpec((1, MOD_ROWS, ADA_COLS), lambda l, j: (l, 0, j)),
        out_shape=jax.ShapeDtypeStruct((DEPTH, MOD_ROWS, n_cols), _F32),
        compiler_params=_params(2),
        name="ada_modulation",
    )(c_all, w_ada, b_ada.reshape(DEPTH, 1, n_cols))


_W_GROUPS = (
    (OFF_QA, ATTN_WIDTH, ((F_QA, None),)),
    (OFF_KA, 2 * KV_WIDTH, ((None, K_ATT), (F_VA, None))),
    (OFF_QR, RET_WIDTH, ((F_QR, None),)),
    (OFF_KR, RET_WIDTH, ((None, K_RET),)),
    (OFF_VR, RET_WIDTH, ((F_VR, None),)),
    (OFF_GR, RET_WIDTH, ((F_GR, None),)),
)
Q_SCALE = ATTN_HEAD_DIM ** -0.5 * LOG2_E


def _modulated_norm(x, gain, shift, scale):
    ms = jnp.mean(x * x, axis=-1, keepdims=True)
    return (x * lax.rsqrt(ms + EPS) * gain) * (1.0 + scale) + shift


def _inproj_kernel(x_ref, mod_ref, g_ref, wt_ref, k_ref, f_ref):
    for t in range(PROJ_TILE // PROJ_SUB):
        rows = slice(t * PROJ_SUB, (t + 1) * PROJ_SUB)
        h = _modulated_norm(x_ref[0, rows, :], g_ref[0], mod_ref[0, 0:1, :], mod_ref[0, 1:2, :]).astype(_BF16)
        for w_off, w_rows, dests in _W_GROUPS:
            p = lax.dot_general(wt_ref[0, w_off:w_off + w_rows, :], h, _NT, preferred_element_type=_F32)
            if w_off == OFF_QA:
                p = p * Q_SCALE
            width = w_rows // len(dests)
            for i, (f_row, k_col) in enumerate(dests):
                part = p[i * width:(i + 1) * width, :]
                if f_row is None:
                    k_ref[0, rows, k_col:k_col + width] = part.T.astype(_BF16)
                else:
                    for c in range(PROJ_SUB // BLOCK):
                        f_ref[0, t * (PROJ_SUB // BLOCK) + c, f_row:f_row + width, :] = (
                            part[:, c * BLOCK:(c + 1) * BLOCK].astype(_BF16))


def _in_projection(x, mod, gains, w_t, layer):
    B, S, _ = x.shape
    assert S % PROJ_TILE == 0 and PROJ_TILE % PROJ_SUB == 0 and PROJ_SUB % BLOCK == 0
    blocks = PROJ_TILE // BLOCK
    this_layer = lambda b, i: (layer, 0, 0)
    return pl.pallas_call(
        _inproj_kernel,
        grid=(B, S // PROJ_TILE),
        in_specs=[
            pl.BlockSpec((1, PROJ_TILE, D_MODEL), lambda b, i: (b, i, 0)),
            pl.BlockSpec((1, N_MOD, D_MODEL), lambda b, i: (b, 0, 0)),
            pl.BlockSpec((1, 1, D_MODEL), this_layer),
            pl.BlockSpec((1, IN_WIDTH, D_MODEL), this_layer),
        ],
        out_specs=[
            pl.BlockSpec((1, PROJ_TILE, K_COLS), lambda b, i: (b, i, 0)),
            pl.BlockSpec((1, blocks, F_ROWS, BLOCK), lambda b, i: (b, i, 0, 0)),
        ],
        out_shape=[
            jax.ShapeDtypeStruct((B, S, K_COLS), _BF16),
            jax.ShapeDtypeStruct((B, S // BLOCK, F_ROWS, BLOCK), _BF16),
        ],
        compiler_params=_params(2),
        name="in_projection",
    )(x, mod, gains, w_t)


def _log_decay(decay):
    return jnp.minimum(decay, 0.0) - jnp.log1p(jnp.exp(-jnp.abs(decay)))


def _state_kernel(dec_ref, k_ref, v_ref, rf_ref, rb_ref, s_ref, kvb_ref, ktab_ref, ctab_ref, *, n_chunks):
    s_ref[...] = jnp.zeros_like(s_ref)
    pos = lax.broadcasted_iota(jnp.int32, (BLOCK, 1), 0).astype(_F32)
    for h in range(RET_HEADS):
        cols = slice(h * RET_HEAD_DIM, (h + 1) * RET_HEAD_DIM)
        for d in range(2):
            lg = _log_decay(dec_ref[d:d + 1, cols])
            expo = (BLOCK - 1.0 - pos) if d == 0 else pos
            ktab_ref[h, :, d * RET_HEAD_DIM:(d + 1) * RET_HEAD_DIM] = (RET_HEAD_DIM ** -0.5) * jnp.exp(expo * lg)
            ctab_ref[d, h] = jnp.broadcast_to(jnp.exp(float(BLOCK) * lg), (8, RET_HEAD_DIM))

    def forward(c, carry):
        row0 = pl.multiple_of(c * BLOCK, BLOCK)
        for h in range(RET_HEADS):
            cols = slice(h * RET_HEAD_DIM, (h + 1) * RET_HEAD_DIM)
            k_cols = slice(K_RET + h * RET_HEAD_DIM, K_RET + (h + 1) * RET_HEAD_DIM)
            k = k_ref[0, pl.ds(row0, BLOCK), k_cols].astype(_F32)
            k_dec = (jnp.concatenate([k, k], axis=1) * ktab_ref[h]).astype(_BF16)
            kv_t = jnp.dot(v_ref[0, c, cols, :], k_dec, preferred_element_type=_F32)
            state = s_ref[0, h]
            rf_ref[0, c, h] = state.astype(_BF16)
            s_ref[0, h] = state * ctab_ref[0, h, 0:1, :] + kv_t[:, :RET_HEAD_DIM]
            kvb_ref[c, h] = kv_t[:, RET_HEAD_DIM:]
        return carry

    def backward(i, carry):
        c = n_chunks - 1 - i
        for h in range(RET_HEADS):
            state = s_ref[1, h]
            rb_ref[0, c, h] = state.astype(_BF16)
            s_ref[1, h] = state * ctab_ref[1, h, 0:1, :] + kvb_ref[c, h]
        return carry

    lax.fori_loop(0, n_chunks, forward, 0, unroll=STATE_UNROLL)
    lax.fori_loop(0, n_chunks, backward, 0, unroll=STATE_UNROLL)


def _retention_states(k_nat, feat, decays):
    B, S, _ = k_nat.shape
    N = S // BLOCK
    assert N % STATE_UNROLL == 0
    st_shape = jax.ShapeDtypeStruct((B, N, RET_HEADS, RET_HEAD_DIM, RET_HEAD_DIM), _BF16)
    st_spec = pl.BlockSpec((1, N, RET_HEADS, RET_HEAD_DIM, RET_HEAD_DIM), lambda b: (b, 0, 0, 0, 0))
    return pl.pallas_call(
        functools.partial(_state_kernel, n_chunks=N),
        grid=(B,),
        in_specs=[
            pl.BlockSpec((2, RET_WIDTH), lambda b: (0, 0)),
            pl.BlockSpec((1, S, K_COLS), lambda b: (b, 0, 0)),
            pl.BlockSpec((1, N, RET_WIDTH, BLOCK), lambda b: (b, 0, F_VR // RET_WIDTH, 0)),
        ],
        out_specs=[st_spec, st_spec],
        out_shape=[st_shape, st_shape],
        scratch_shapes=[
            pltpu.VMEM((2, RET_HEADS, RET_HEAD_DIM, RET_HEAD_DIM), _F32),
            pltpu.VMEM((N, RET_HEADS, RET_HEAD_DIM, RET_HEAD_DIM), _F32),
            pltpu.VMEM((RET_HEADS, BLOCK, 2 * RET_HEAD_DIM), _F32),
            pltpu.VMEM((2, RET_HEADS, 8, RET_HEAD_DIM), _F32),
        ],
        compiler_params=_params(1),
        name="retention_states",
    )(decays, k_nat, feat)


def _mix_block(sub, variant, sink_ref, feat_ref, k_ref, k_blocks, v_blocks, rf_ref, rb_ref, o_ref,
               bias_ref, dcomb_ref, qdec_ref, s_ref, p_ref):
    tok = slice(sub * BLOCK, (sub + 1) * BLOCK)
    kband = jnp.concatenate(k_blocks[sub:sub + 3], axis=0)
    vband = jnp.concatenate(v_blocks[sub:sub + 3], axis=1)

    zeros = jnp.zeros((ATTN_HEAD_DIM, ATTN_GROUP * BLOCK), _BF16)
    q_rows = []
    for kh in range(ATTN_KV_HEADS):
        tiles = [feat_ref[0, sub, F_QA + hh * ATTN_HEAD_DIM:F_QA + (hh + 1) * ATTN_HEAD_DIM, :]
                 for hh in range(kh * ATTN_GROUP, (kh + 1) * ATTN_GROUP)]
        mine = jnp.concatenate(tiles, axis=1)
        q_rows.append(jnp.concatenate([mine, zeros] if kh == 0 else [zeros, mine], axis=1))
    q_bd = jnp.concatenate(q_rows, axis=0)
    scores = jnp.dot(kband, q_bd, preferred_element_type=_F32)
    for hh in range(ATTN_HEADS):
        s_ref[sub, hh] = scores[:, hh * BLOCK:(hh + 1) * BLOCK]

    ret_parts = {}

    def start_retention(h):
        k_cols = slice(K_RET + h * RET_HEAD_DIM, K_RET + (h + 1) * RET_HEAD_DIM)
        q_t = feat_ref[0, sub, F_QR + h * RET_HEAD_DIM:F_QR + (h + 1) * RET_HEAD_DIM, :]
        s_t = jnp.dot(k_ref[0, tok, k_cols], q_t, preferred_element_type=_F32)
        cross = jnp.dot(rf_ref[0, sub, h], q_t, preferred_element_type=_F32) * qdec_ref[0, h, 0:1, :]
        cross = cross + jnp.dot(rb_ref[0, sub, h], q_t, preferred_element_type=_F32) * qdec_ref[1, h, 0:1, :]
        ret_parts[h] = ((s_t * dcomb_ref[h]).astype(_BF16), cross)

    def finish_retention(h):
        a_t, cross = ret_parts[h]
        v_t = feat_ref[0, sub, F_VR + h * RET_HEAD_DIM:F_VR + (h + 1) * RET_HEAD_DIM, :]
        y = jnp.dot(v_t, a_t, preferred_element_type=_F32) + cross
        yc = y - jnp.mean(y, axis=0, keepdims=True)
        yn = yc * lax.rsqrt(jnp.mean(yc * yc, axis=0, keepdims=True) + EPS)
        gate = feat_ref[0, sub, F_GR + h * RET_HEAD_DIM:F_GR + (h + 1) * RET_HEAD_DIM, :].astype(_F32)
        lanes = slice(ATTN_WIDTH + h * RET_HEAD_DIM, ATTN_WIDTH + (h + 1) * RET_HEAD_DIM)
        o_ref[0, tok, lanes] = (gate * jax.nn.sigmoid(gate) * yn).T.astype(_BF16)

    sink_terms = []
    ones = jnp.ones((_BF16_ROWS, BAND), _BF16)

    def softmax_head(hh):
        st = s_ref[sub, hh] + bias_ref[variant, hh]
        sink = sink_ref[hh] * LOG2_E
        m = jnp.maximum(jnp.max(st, axis=0, keepdims=True), sink)
        p_ref[sub, hh] = jnp.exp2(st - m).astype(_BF16)
        sink_terms.append(jnp.exp2(sink - m))

    def attend(kh):
        v_t = jnp.concatenate([vband[kh * ATTN_HEAD_DIM:(kh + 1) * ATTN_HEAD_DIM, :], ones], axis=0)
        probs = jnp.concatenate(
            [p_ref[sub, hh] for hh in range(kh * ATTN_GROUP, (kh + 1) * ATTN_GROUP)], axis=1)
        o_t = jnp.dot(v_t, probs, preferred_element_type=_F32)
        for pair in range(ATTN_GROUP // 2):
            scaled = []
            for g in (2 * pair, 2 * pair + 1):
                q_lanes = slice(g * BLOCK, (g + 1) * BLOCK)
                denom = o_t[ATTN_HEAD_DIM:ATTN_HEAD_DIM + 1, q_lanes] + sink_terms[kh * ATTN_GROUP + g]
                scaled.append(o_t[:ATTN_HEAD_DIM, q_lanes] * (1.0 / denom))
            both = jnp.concatenate(scaled, axis=0)
            first = (kh * ATTN_GROUP + 2 * pair) * ATTN_HEAD_DIM
            o_ref[0, tok, first:first + 2 * ATTN_HEAD_DIM] = both.T.astype(_BF16)

    for hh in range(ATTN_HEADS):
        if hh % 2 == 0:
            start_retention(hh // 2)
        softmax_head(hh)
        if hh % 2 == 1:
            finish_retention(hh // 2)
        if hh == ATTN_GROUP - 1:
            attend(0)
    attend(1)


def _mix_kernel(sink_ref, dec_ref, feat_ref, k_ref, kp_ref, kn_ref, vp_ref, vn_ref, rf_ref, rb_ref, o_ref,
                bias_ref, dcomb_ref, qdec_ref, s_ref, p_ref, *, n_steps):
    step = pl.program_id(1)

    @pl.when(step == 0)
    def _():
        jj = lax.broadcasted_iota(jnp.int32, (BAND, BLOCK), 0)
        ii = lax.broadcasted_iota(jnp.int32, (BAND, BLOCK), 1)
        dist = jnp.abs(jj - BLOCK - ii)
        distf = dist.astype(_F32)
        has_prev, has_next = jj >= BLOCK, jj < 2 * BLOCK
        for hh in range(ATTN_HEADS):
            slope = 2.0 ** (-8.0 * (hh + 1.0) / ATTN_HEADS)
            table = jnp.where(dist <= WINDOW, (-slope * LOG2_E) * distf, NEG_INF)
            bias_ref[0, hh] = table
            bias_ref[1, hh] = jnp.where(has_prev, table, NEG_INF)
            bias_ref[2, hh] = jnp.where(has_next, table, NEG_INF)
        rj = lax.broadcasted_iota(jnp.int32, (BLOCK, BLOCK), 0)
        ri = lax.broadcasted_iota(jnp.int32, (BLOCK, BLOCK), 1)
        diff = (ri - rj).astype(_F32)
        posq = lax.broadcasted_iota(jnp.int32, (8, BLOCK), 1).astype(_F32)
        for h in range(RET_HEADS):
            cols = slice(h * RET_HEAD_DIM, (h + 1) * RET_HEAD_DIM)
            lg_f = _log_decay(dec_ref[0:1, cols])
            lg_b = _log_decay(dec_ref[1:2, cols])
            d_f = jnp.exp(jnp.maximum(diff, 0.0) * lg_f)
            d_b = jnp.exp(jnp.maximum(-diff, 0.0) * lg_b)
            dcomb_ref[h] = jnp.where(diff >= 0, d_f, d_b) * (RET_HEAD_DIM ** -0.5)
            qdec_ref[0, h] = jnp.exp((posq + 1.0) * lg_f)
            qdec_ref[1, h] = jnp.exp((float(BLOCK) - posq) * lg_b)

    k_blocks = ([kp_ref[0]]
                + [k_ref[0, i * BLOCK:(i + 1) * BLOCK, K_ATT:K_ATT + KV_WIDTH] for i in range(MIX_BLOCKS)]
                + [kn_ref[0]])
    v_blocks = ([vp_ref[0, 0]]
                + [feat_ref[0, i, F_VA:F_VA + KV_WIDTH, :] for i in range(MIX_BLOCKS)]
                + [vn_ref[0, 0]])
    for sub in range(MIX_BLOCKS):
        variant = 0
        if sub == 0:
            variant = jnp.where(step == 0, 1, variant)
        if sub == MIX_BLOCKS - 1:
            variant = jnp.where(step == n_steps - 1, 2, variant)
        _mix_block(sub, variant, sink_ref, feat_ref, k_ref, k_blocks, v_blocks, rf_ref, rb_ref, o_ref,
                   bias_ref, dcomb_ref, qdec_ref, s_ref, p_ref)


def _token_mixing(k_nat, feat, rf, rb, sink, decays):
    B, S, _ = k_nat.shape
    N = S // BLOCK
    n_steps = N // MIX_BLOCKS
    assert n_steps * MIX_BLOCKS == N and N >= 2
    prev_blk = lambda i: jnp.maximum(i * MIX_BLOCKS - 1, 0)
    next_blk = lambda i: jnp.minimum((i + 1) * MIX_BLOCKS, N - 1)
    k_col, v_row = K_ATT // KV_WIDTH, F_VA // KV_WIDTH
    st_spec = pl.BlockSpec((1, MIX_BLOCKS, RET_HEADS, RET_HEAD_DIM, RET_HEAD_DIM), lambda b, i: (b, i, 0, 0, 0))
    return pl.pallas_call(
        functools.partial(_mix_kernel, n_steps=n_steps),
        grid=(B, n_steps),
        in_specs=[
            pl.BlockSpec(memory_space=pltpu.SMEM),
            pl.BlockSpec((2, RET_WIDTH), lambda b, i: (0, 0)),
            pl.BlockSpec((1, MIX_BLOCKS, F_ROWS, BLOCK), lambda b, i: (b, i, 0, 0)),
            pl.BlockSpec((1, MIX_BLOCKS * BLOCK, K_COLS), lambda b, i: (b, i, 0)),
            pl.BlockSpec((1, BLOCK, KV_WIDTH), lambda b, i: (b, prev_blk(i), k_col)),
            pl.BlockSpec((1, BLOCK, KV_WIDTH), lambda b, i: (b, next_blk(i), k_col)),
            pl.BlockSpec((1, 1, KV_WIDTH, BLOCK), lambda b, i: (b, prev_blk(i), v_row, 0)),
            pl.BlockSpec((1, 1, KV_WIDTH, BLOCK), lambda b, i: (b, next_blk(i), v_row, 0)),
            st_spec, st_spec,
        ],
        out_specs=pl.BlockSpec((1, MIX_BLOCKS * BLOCK, D_MODEL), lambda b, i: (b, i, 0)),
        out_shape=jax.ShapeDtypeStruct((B, S, D_MODEL), _BF16),
        scratch_shapes=[
            pltpu.VMEM((3, ATTN_HEADS, BAND, BLOCK), _F32),
            pltpu.VMEM((RET_HEADS, BLOCK, BLOCK), _F32),
            pltpu.VMEM((2, RET_HEADS, 8, BLOCK), _F32),
            pltpu.VMEM((MIX_BLOCKS, ATTN_HEADS, BAND, BLOCK), _F32),
            pltpu.VMEM((MIX_BLOCKS, ATTN_HEADS, BAND, BLOCK), _BF16),
        ],
        compiler_params=_params(2),
        name="token_mixing",
    )(sink, decays, feat, k_nat, k_nat, k_nat, feat, feat, rf, rb)


def _mlp_kernel(x_ref, mix_ref, mod_ref, g2_ref, gf_ref, wo_ref, w1_ref, w2_ref, o_ref, *, final):
    subs = [slice(t * MLP_SUB, (t + 1) * MLP_SUB) for t in range(ROW_TILE // MLP_SUB)]
    projected = [jnp.dot(mix_ref[0, rows, :], wo_ref[0], preferred_element_type=_F32) for rows in subs]
    for rows, proj in zip(subs, projected):
        x1 = x_ref[0, rows, :] + mod_ref[0, 2:3, :] * proj
        h = _modulated_norm(x1, g2_ref[0], mod_ref[0, 3:4, :], mod_ref[0, 4:5, :]).astype(_BF16)
        acc = jnp.zeros((MLP_SUB, D_MODEL), _F32)
        for c in range(D_FF // FF_CHUNK):
            cols = slice(c * FF_CHUNK, (c + 1) * FF_CHUNK)
            a = jnp.dot(h, w1_ref[0, :, cols], preferred_element_type=_F32)
            a = jnp.square(jnp.maximum(a, 0.0)).astype(_BF16)
            acc = acc + jnp.dot(a, w2_ref[0, cols, :], preferred_element_type=_F32)
        x2 = x1 + mod_ref[0, 5:6, :] * acc
        if final:
            ms = jnp.mean(x2 * x2, axis=-1, keepdims=True)
            x2 = x2 * lax.rsqrt(ms + EPS) * gf_ref[...]
        o_ref[0, rows, :] = x2


def _channel_mixing(x, mix, mod, gains2, gain_final, w_out, w_mlp1, w_mlp2, layer):
    B, S, _ = x.shape
    assert S % ROW_TILE == 0 and ROW_TILE % MLP_SUB == 0
    row = lambda b, i: (b, i, 0)
    this_layer = lambda b, i: (layer, 0, 0)
    resident = dict(pipeline_mode=pl.Buffered(1))
    return pl.pallas_call(
        functools.partial(_mlp_kernel, final=(layer == DEPTH - 1)),
        grid=(B, S // ROW_TILE),
        in_specs=[
            pl.BlockSpec((1, ROW_TILE, D_MODEL), row),
            pl.BlockSpec((1, ROW_TILE, D_MODEL), row),
            pl.BlockSpec((1, N_MOD, D_MODEL), lambda b, i: (b, 0, 0)),
            pl.BlockSpec((1, 1, D_MODEL), this_layer),
            pl.BlockSpec((1, D_MODEL), lambda b, i: (0, 0)),
            pl.BlockSpec((1, D_MODEL, D_MODEL), this_layer, **resident),
            pl.BlockSpec((1, D_MODEL, D_FF), this_layer, **resident),
            pl.BlockSpec((1, D_FF, D_MODEL), this_layer, **resident),
        ],
        out_specs=pl.BlockSpec((1, ROW_TILE, D_MODEL), row),
        out_shape=jax.ShapeDtypeStruct((B, S, D_MODEL), _F32),
        compiler_params=_params(2),
        name="channel_mixing",
    )(x, mix, mod, gains2, gain_final, w_out, w_mlp1, w_mlp2)


def _run_trunk(x, mod_rows, norm1_g, w_t, attn_sink, decays, w_out, norm2_g, w_mlp1, w_mlp2, final_g):
    B = x.shape[0]
    for l in range(DEPTH):
        mod = mod_rows[l].reshape(B, N_MOD, D_MODEL)
        k_nat, feat = _in_projection(x, mod, norm1_g, w_t, l)
        rf, rb = _retention_states(k_nat, feat, decays[l])
        mix = _token_mixing(k_nat, feat, rf, rb, attn_sink[l], decays[l])
        x = _channel_mixing(x, mix, mod, norm2_g, final_g, w_out, w_mlp1, w_mlp2, l)
    return x


def kernel(x_prompt, x_sample, c_prompt, c_sample, w_ada, b_ada, norm1_g, w_in, attn_sink,
           ret_decay_fwd, ret_decay_bwd, w_out, norm2_g, w_mlp1, w_mlp2, final_g):
    bp, bs = x_prompt.shape[0], x_sample.shape[0]
    c_all = jnp.concatenate(
        [c_prompt, c_sample, jnp.zeros((MOD_ROWS - bp - bs, D_MODEL), _F32)], axis=0)
    mod_all = _ada_modulation(c_all, w_ada, b_ada)
    decays = jnp.stack([jnp.repeat(ret_decay_fwd, RET_HEAD_DIM, axis=-1),
                        jnp.repeat(ret_decay_bwd, RET_HEAD_DIM, axis=-1)], axis=1)
    w_t = w_in.astype(_BF16).transpose(0, 2, 1)
    shared = (norm1_g.reshape(DEPTH, 1, D_MODEL), w_t, attn_sink, decays, w_out.astype(_BF16),
              norm2_g.reshape(DEPTH, 1, D_MODEL),
              w_mlp1.astype(_BF16), w_mlp2.astype(_BF16), final_g.reshape(1, D_MODEL))
    y_prompt = _run_trunk(x_prompt, mod_all[:, :bp], *shared)
    y_sample = _run_trunk(x_sample, mod_all[:, bp:bp + bs], *shared)
    return (y_prompt, y_sample)
```
